```python
import math
import jax, jax.numpy as jnp
from jax import lax
import numpy as np

D_MODEL = 2048
BATCH = 16
SEQ = 256
DEPTH = 4
DEC_BATCH = 2
DEC_SEQ = 1024
PAST_LEN = 256

GRID_W = 64
N_MIXERS = 3
N_SSM_LAYERS = (DEPTH + 2) // 3
N_DIFF_LAYERS = (DEPTH + 1) // 3
N_WIN_LAYERS = DEPTH // 3
N_MOD = 9
D_FF = 5632
FFN_RES = 0.5
EPS = 1e-6
ROPE_BASE = 10000.0
QBLK = 128

SSM_EXPAND = 2
D_INNER = SSM_EXPAND * D_MODEL
SSM_HEAD_P = 64
SSM_HEADS = D_INNER // SSM_HEAD_P
SSM_GROUPS = 8
SSM_HPG = SSM_HEADS // SSM_GROUPS
D_STATE = 128
D_CONV = 3
CONV_CH = D_INNER + 2 * SSM_GROUPS * D_STATE
SSM_IN = D_INNER + CONV_CH + 2 * SSM_HEADS
CHUNK = 128

DIFF_HEADS = 8
DIFF_DH = D_MODEL // (2 * DIFF_HEADS)
DIFF_QKV = 4 * DIFF_HEADS * DIFF_DH + DIFF_HEADS * 2 * DIFF_DH

WIN_HEADS = 16
WIN_KV = 4
WIN_QPK = WIN_HEADS // WIN_KV
WIN_DH = D_MODEL // WIN_HEADS
WINDOW = 128
WIN_QKV = (WIN_HEADS + 2 * WIN_KV) * WIN_DH

kernel_name = "hybrid_flow_ssd_diff_swa_step"


def _rmsnorm(x, g):
    xf = x.astype(jnp.float32)
    y = xf * lax.rsqrt(jnp.mean(xf * xf, axis=-1, keepdims=True) + EPS)
    return (y * g.astype(jnp.float32)).astype(x.dtype)


def _swiglu(h, w_in, w_out):
    gate, up = jnp.split(h @ w_in, 2, axis=-1)
    return (jax.nn.silu(gate) * up) @ w_out


def _modulation(cvec, w, b):
    m = jax.nn.silu(cvec) @ w + b
    return [t[:, None, :] for t in jnp.split(m, N_MOD, axis=-1)]


def _modnorm(x, g, shift, scale):
    return _rmsnorm(x, g) * (1.0 + scale) + shift


def _ffn_sub(x, shift, scale, gate, g, w_in, w_out):
    return x + FFN_RES * gate * _swiglu(_modnorm(x, g, shift, scale), w_in, w_out)


def _to_blocks(t, blk):
    b, L = t.shape[:2]
    return jnp.moveaxis(t.reshape((b, L // blk, blk) + t.shape[2:]), 1, 0)


def _from_blocks(t):
    t = jnp.moveaxis(t, 0, 1)
    return t.reshape((t.shape[0], t.shape[1] * t.shape[2]) + t.shape[3:])


def _axial_angles(L, d):
    rows = L // GRID_W
    row = jnp.repeat(jnp.arange(rows, dtype=jnp.float32), GRID_W)
    col = jnp.tile(jnp.arange(GRID_W, dtype=jnp.float32), rows)
    nf = d // 4
    inv = ROPE_BASE ** (-jnp.arange(nf, dtype=jnp.float32) / nf)
    return jnp.concatenate([row[:, None] * inv, col[:, None] * inv], axis=-1)


def _apply_axial_rope(x, ang):
    d = x.shape[-1]
    h, qd = d // 2, d // 4
    ang = ang.reshape((ang.shape[0],) + (1,) * (x.ndim - 3) + (h,))
    cos, sin = jnp.cos(ang), jnp.sin(ang)
    xf = x.astype(jnp.float32)
    parts = []
    for a in range(2):
        xa = xf[..., a * h:(a + 1) * h]
        ca, sa = cos[..., a * qd:(a + 1) * qd], sin[..., a * qd:(a + 1) * qd]
        x1, x2 = xa[..., :qd], xa[..., qd:]
        parts += [x1 * ca - x2 * sa, x2 * ca + x1 * sa]
    return jnp.concatenate(parts, axis=-1).astype(x.dtype)


def _dwconv(x, w):
    C = x.shape[-1]
    return lax.conv_general_dilated(x, w[:, None, :], window_strides=(1,),
                                    padding=[(D_CONV // 2, D_CONV // 2)],
                                    dimension_numbers=('NWC', 'WIO', 'NWC'),
                                    feature_group_count=C)


def _ssd(x, dt, a, bm, cm, h0):
    b, L, G, R, P = x.shape
    N = bm.shape[-1]
    nc = L // CHUNK
    x = x.reshape(b, nc, CHUNK, G, R, P)
    dt = dt.reshape(b, nc, CHUNK, G, R)
    bm = bm.reshape(b, nc, CHUNK, G, N)
    cm = cm.reshape(b, nc, CHUNK, G, N)
    acum = jnp.cumsum(dt * a, axis=2)
    seg = acum[:, :, :, None] - acum[:, :, None, :]
    causal = jnp.tril(jnp.ones((CHUNK, CHUNK), dtype=bool))[:, :, None, None]
    decay = jnp.exp(jnp.where(causal, seg, -jnp.inf))
    xdt = x * dt[..., None]
    cb = jnp.einsum('bcign,bcjgn->bcijg', cm, bm)
    y_diag = jnp.einsum('bcijg,bcijgr,bcjgrp->bcigrp', cb, decay, xdt)
    decay_end = jnp.exp(acum[:, :, -1:] - acum)
    chunk_states = jnp.einsum('bcjgn,bcjgr,bcjgrp->bcgrpn', bm, decay_end, xdt)
    chunk_decay = jnp.exp(acum[:, :, -1])

    def step(h, inp):
        s, dcy = inp
        return h * dcy[..., None, None] + s, h

    h_final, h_prev = lax.scan(step, h0, (jnp.moveaxis(chunk_states, 1, 0),
                                          jnp.moveaxis(chunk_decay, 1, 0)))
    h_prev = jnp.moveaxis(h_prev, 0, 1)
    y_off = jnp.einsum('bcign,bcigr,bcgrpn->bcigrp', cm, jnp.exp(acum), h_prev)
    return (y_diag + y_off).reshape(b, L, G, R, P), h_final


def _ssm_mixer(h, h0_f, h0_b, w_in, conv_w, conv_b, dt_bias, a_log, d_skip, norm_g, w_out):
    f32 = jnp.float32
    b, L, _ = h.shape
    z, xbc, dtr = jnp.split(h @ w_in, [D_INNER, D_INNER + CONV_CH], axis=-1)
    xbc = jax.nn.silu(_dwconv(xbc, conv_w) + conv_b)
    xs, bm, cm = jnp.split(xbc, [D_INNER, D_INNER + SSM_GROUPS * D_STATE], axis=-1)
    xs = xs.reshape(b, L, SSM_GROUPS, SSM_HPG, SSM_HEAD_P).astype(f32)
    bm = bm.reshape(b, L, SSM_GROUPS, D_STATE).astype(f32)
    cm = cm.reshape(b, L, SSM_GROUPS, D_STATE).astype(f32)
    dt = jax.nn.softplus(dtr.reshape(b, L, 2, SSM_GROUPS, SSM_HPG).astype(f32)
                         + dt_bias.astype(f32).reshape(2, SSM_GROUPS, SSM_HPG))
    a = -jnp.exp(a_log.astype(f32)).reshape(2, SSM_GROUPS, SSM_HPG)
    shp = (b, SSM_GROUPS, SSM_HPG, SSM_HEAD_P, D_STATE)
    flip = lambda t: jnp.flip(t, axis=1)
    y_f, hf = _ssd(xs, dt[:, :, 0], a[0], bm, cm, h0_f.astype(f32).reshape(shp))
    y_b, hb = _ssd(flip(xs), flip(dt[:, :, 1]), a[1], flip(bm), flip(cm),
                   h0_b.astype(f32).reshape(shp))
    y = y_f + flip(y_b) + d_skip.astype(f32).reshape(SSM_GROUPS, SSM_HPG)[:, :, None] * xs
    y = y.reshape(b, L, D_INNER).astype(h.dtype)
    y = _rmsnorm(y * jax.nn.silu(z), norm_g)
    st = (b, SSM_HEADS, SSM_HEAD_P, D_STATE)
    return y @ w_out, hf.reshape(st).astype(h.dtype), hb.reshape(st).astype(h.dtype)


def _diff_qkv(h, w_qkv, ang):
    b, L, _ = h.shape
    q, k, v = jnp.split(h @ w_qkv, [2 * DIFF_HEADS * DIFF_DH, 4 * DIFF_HEADS * DIFF_DH], axis=-1)
    q = q.reshape(b, L, 2, DIFF_HEADS, DIFF_DH)
    k = k.reshape(b, L, 2, DIFF_HEADS, DIFF_DH)
    v = v.reshape(b, L, DIFF_HEADS, 2 * DIFF_DH)
    if ang is not None:
        q, k = _apply_axial_rope(q, ang), _apply_axial_rope(k, ang)
    return q, k, v


def _diff_attend(q, k_all, v_all, lam):
    scale = DIFF_DH ** -0.5

    def blk(qb):
        s = jnp.einsum('bqmhd,bkmhd->bmhqk', qb, k_all).astype(jnp.float32) * scale
        p = jax.nn.softmax(s, axis=-1)
        p = p[:, 0] - lam * p[:, 1]
        return jnp.einsum('bhqk,bkhe->bqhe', p.astype(v_all.dtype), v_all)

    return _from_blocks(lax.map(blk, _to_blocks(q, QBLK)))


def _diff_out(o, subln_g, w_out, lambda_init):
    o = _rmsnorm(o, subln_g) * (1.0 - lambda_init)
    b, L = o.shape[:2]
    return o.reshape(b, L, DIFF_HEADS * 2 * DIFF_DH) @ w_out


def _win_qkv(h, w_qkv, ang):
    b, L, _ = h.shape
    q, k, v = jnp.split(h @ w_qkv, [WIN_HEADS * WIN_DH, (WIN_HEADS + WIN_KV) * WIN_DH], axis=-1)
    q = q.reshape(b, L, WIN_KV, WIN_QPK, WIN_DH)
    k = k.reshape(b, L, WIN_KV, WIN_DH)
    v = v.reshape(b, L, WIN_KV, WIN_DH)
    if ang is not None:
        q, k = _apply_axial_rope(q, ang), _apply_axial_rope(k, ang)
    return q, k, v


def _sink_attend_block(qb, k_all, v_all, valid, sink):
    s = jnp.einsum('bqgrd,bkgd->bgrqk', qb, k_all).astype(jnp.float32) * (WIN_DH ** -0.5)
    if valid is not None:
        s = jnp.where(valid, s, -jnp.inf)
    sk = jnp.broadcast_to(sink.astype(jnp.float32)[None, :, :, None, None], s.shape[:-1] + (1,))
    p = jax.nn.softmax(jnp.concatenate([s, sk], axis=-1), axis=-1)[..., :-1]
    return jnp.einsum('bgrqk,bkgd->bqgrd', p.astype(v_all.dtype), v_all)


def _win_context(q, kc, vc, sink):
    blk = lambda qb: _sink_attend_block(qb, kc, vc, None, sink)
    return _from_blocks(lax.map(blk, _to_blocks(q, QBLK)))


def _win_latent(q, k, v, kc, vc, sink):
    b, L = q.shape[:2]
    nb = L // WINDOW
    Lc = kc.shape[1]

    def band(t):
        tp = jnp.pad(t, ((0, 0), (WINDOW, WINDOW), (0, 0), (0, 0)))
        tp = tp.reshape((b, nb + 2, WINDOW) + t.shape[2:])
        return jnp.moveaxis(jnp.concatenate([tp[:, :-2], tp[:, 1:-1], tp[:, 2:]], axis=2), 1, 0)

    n = jnp.arange(nb)[:, None, None]
    qpos = n * WINDOW + jnp.arange(WINDOW)[None, :, None]
    kpos = (n - 1) * WINDOW + jnp.arange(3 * WINDOW)[None, None, :]
    band_ok = (jnp.abs(qpos - kpos) <= WINDOW) & (kpos >= 0) & (kpos < L)
    valid = jnp.concatenate([jnp.ones((nb, WINDOW, Lc), dtype=bool), band_ok], axis=-1)

    def blk(args):
        qb, kb, vb, ok = args
        return _sink_attend_block(qb, jnp.concatenate([kc, kb], axis=1),
                                  jnp.concatenate([vc, vb], axis=1), ok, sink)

    return _from_blocks(lax.map(blk, (_to_blocks(q, WINDOW), band(k), band(v), valid)))


def setup_inputs(seed: int = 0) -> dict:
    key = jax.random.key(seed)
    ks = iter(jax.random.split(key, 48))
    f32 = jnp.float32
    D = D_MODEL
    nrm = lambda shape, s: jax.random.normal(next(ks), shape, f32) * s
    gain = lambda shape: 1.0 + nrm(shape, 0.05)
    unif = lambda shape, lo, hi: jax.random.uniform(next(ks), shape, f32, lo, hi)
    dt0 = jnp.exp(unif((N_SSM_LAYERS, 2, SSM_HEADS), math.log(1e-3), math.log(1e-1)))
    return {
        "x_prompt": nrm((BATCH, SEQ, D), 1.0),
        "x_sample": nrm((DEC_BATCH, DEC_SEQ, D), 1.0),
        "state_l0_fwd": nrm((DEC_BATCH, SSM_HEADS, SSM_HEAD_P, D_STATE), 0.5),
        "state_l0_bwd": nrm((DEC_BATCH, SSM_HEADS, SSM_HEAD_P, D_STATE), 0.5),
        "cache_l1_k": nrm((DEC_BATCH, PAST_LEN, 2, DIFF_HEADS, DIFF_DH), 1.0),
        "cache_l1_v": nrm((DEC_BATCH, PAST_LEN, DIFF_HEADS, 2 * DIFF_DH), 1.0),
        "cache_l2_k": nrm((DEC_BATCH, PAST_LEN, WIN_KV, WIN_DH), 1.0),
        "cache_l2_v": nrm((DEC_BATCH, PAST_LEN, WIN_KV, WIN_DH), 1.0),
        "state_l3_fwd": nrm((DEC_BATCH, SSM_HEADS, SSM_HEAD_P, D_STATE), 0.5),
        "state_l3_bwd": nrm((DEC_BATCH, SSM_HEADS, SSM_HEAD_P, D_STATE), 0.5),
        "c": nrm((DEC_BATCH, D), 1.0),
        "c_ctx": nrm((D,), 1.0),
        "norm_g": gain((DEPTH, 3, D)),
        "w_ada": nrm((DEPTH, D, N_MOD * D), 0.5 * D ** -0.5),
        "b_ada": nrm((DEPTH, N_MOD * D), 0.02),
        "ffn1_w_in": nrm((DEPTH, D, 2 * D_FF), D ** -0.5),
        "ffn1_w_out": nrm((DEPTH, D_FF, D), D_FF ** -0.5),
        "ffn2_w_in": nrm((DEPTH, D, 2 * D_FF), D ** -0.5),
        "ffn2_w_out": nrm((DEPTH, D_FF, D), D_FF ** -0.5),
        "ssm_w_in": nrm((N_SSM_LAYERS, D, SSM_IN), D ** -0.5),
        "ssm_conv_w": nrm((N_SSM_LAYERS, D_CONV, CONV_CH), D_CONV ** -0.5),
        "ssm_conv_b": nrm((N_SSM_LAYERS, CONV_CH), 0.02),
        "ssm_dt_bias": dt0 + jnp.log(-jnp.expm1(-dt0)),
        "ssm_a_log": jnp.log(unif((N_SSM_LAYERS, 2, SSM_HEADS), 1.0, 16.0)),
        "ssm_d": gain((N_SSM_LAYERS, SSM_HEADS)),
        "ssm_norm_g": gain((N_SSM_LAYERS, D_INNER)),
        "ssm_w_out": nrm((N_SSM_LAYERS, D_INNER, D), D_INNER ** -0.5),
        "diff_w_qkv": nrm((N_DIFF_LAYERS, D, DIFF_QKV), D ** -0.5),
        "diff_lambda": nrm((N_DIFF_LAYERS, 4, DIFF_DH), 0.1),
        "diff_subln_g": gain((N_DIFF_LAYERS, 2 * DIFF_DH)),
        "diff_w_out": nrm((N_DIFF_LAYERS, DIFF_HEADS * 2 * DIFF_DH, D), (DIFF_HEADS * 2 * DIFF_DH) ** -0.5),
        "win_w_qkv": nrm((N_WIN_LAYERS, D, WIN_QKV), D ** -0.5),
        "win_sink": nrm((N_WIN_LAYERS, WIN_HEADS), 0.5),
        "win_w_out": nrm((N_WIN_LAYERS, WIN_HEADS * WIN_DH, D), (WIN_HEADS * WIN_DH) ** -0.5),
        "final_norm_g": gain((D,)),
    }


def reference(x_prompt, x_sample, state_l0_fwd, state_l0_bwd, cache_l1_k, cache_l1_v,
              cache_l2_k, cache_l2_v, state_l3_fwd, state_l3_bwd, c, c_ctx,
              norm_g, w_ada, b_ada, ffn1_w_in, ffn1_w_out, ffn2_w_in, ffn2_w_out,
              ssm_w_in, ssm_conv_w, ssm_conv_b, ssm_dt_bias, ssm_a_log, ssm_d, ssm_norm_g, ssm_w_out,
              diff_w_qkv, diff_lambda, diff_subln_g, diff_w_out,
              win_w_qkv, win_sink, win_w_out, final_norm_g):
    caches = [(state_l0_fwd, state_l0_bwd), (cache_l1_k, cache_l1_v),
              (cache_l2_k, cache_l2_v), (state_l3_fwd, state_l3_bwd)]
    L_lat = x_sample.shape[1]
    ang_diff = _axial_angles(L_lat, DIFF_DH)
    ang_win = _axial_angles(L_lat, WIN_DH)
    xp, xs = x_prompt, x_sample
    cctx = c_ctx[None, :]
    new_state = []
    for i in range(DEPTH):
        m, j = i % N_MIXERS, i // N_MIXERS
        mp = _modulation(cctx, w_ada[i], b_ada[i])
        ms = _modulation(c, w_ada[i], b_ada[i])
        xp = _ffn_sub(xp, mp[0], mp[1], mp[2], norm_g[i, 0], ffn1_w_in[i], ffn1_w_out[i])
        xs = _ffn_sub(xs, ms[0], ms[1], ms[2], norm_g[i, 0], ffn1_w_in[i], ffn1_w_out[i])
        hp = _modnorm(xp, norm_g[i, 1], mp[3], mp[4])
        hs = _modnorm(xs, norm_g[i, 1], ms[3], ms[4])
        cache_a, cache_b = caches[i]
        if m == 0:
            prm = (ssm_w_in[j], ssm_conv_w[j], ssm_conv_b[j], ssm_dt_bias[j], ssm_a_log[j],
                   ssm_d[j], ssm_norm_g[j], ssm_w_out[j])
            z0 = jnp.zeros((hp.shape[0], SSM_HEADS, SSM_HEAD_P, D_STATE), hp.dtype)
            yp, hf, hb = _ssm_mixer(hp, z0, z0, *prm)
            ys, _, _ = _ssm_mixer(hs, cache_a, cache_b, *prm)
            new_state += [hf, hb]
        elif m == 1:
            lambda_init = 0.8 - 0.6 * math.exp(-0.3 * i)
            lv = diff_lambda[j].astype(jnp.float32)
            lam = (jnp.exp(jnp.sum(lv[0] * lv[1])) - jnp.exp(jnp.sum(lv[2] * lv[3]))
                   + lambda_init)
            qp, kp, vp = _diff_qkv(hp, diff_w_qkv[j], None)
            yp = _diff_out(_diff_attend(qp, kp, vp, lam), diff_subln_g[j], diff_w_out[j], lambda_init)
            ql, kl, vl = _diff_qkv(hs, diff_w_qkv[j], ang_diff)
            k_all = jnp.concatenate([cache_a.astype(kl.dtype), kl], axis=1)
            v_all = jnp.concatenate([cache_b.astype(vl.dtype), vl], axis=1)
            ys = _diff_out(_diff_attend(ql, k_all, v_all, lam), diff_subln_g[j], diff_w_out[j], lambda_init)
            new_state += [kp, vp]
        else:
            sink = win_sink[j].reshape(WIN_KV, WIN_QPK)
            qp, kp, vp = _win_qkv(hp, win_w_qkv[j], None)
            op = _win_context(qp, kp, vp, sink)
            yp = op.reshape(op.shape[0], op.shape[1], WIN_HEADS * WIN_DH) @ win_w_out[j]
            ql, kl, vl = _win_qkv(hs, win_w_qkv[j], ang_win)
            ol = _win_latent(ql, kl, vl, cache_a.astype(kl.dtype), cache_b.astype(vl.dtype), sink)
            ys = ol.reshape(ol.shape[0], ol.shape[1], WIN_HEADS * WIN_DH) @ win_w_out[j]
            new_state += [kp, vp]
        xp = xp + mp[5] * yp
        xs = xs + ms[5] * ys
        xp = _ffn_sub(xp, mp[6], mp[7], mp[8], norm_g[i, 2], ffn2_w_in[i], ffn2_w_out[i])
        xs = _ffn_sub(xs, ms[6], ms[7], ms[8], norm_g[i, 2], ffn2_w_in[i], ffn2_w_out[i])
    y_prompt = _rmsnorm(xp, final_norm_g)
    y_sample = _rmsnorm(xs, final_norm_g)
    s0f, s0b, k1, v1, k2, v2, s3f, s3b = new_state
    return (y_prompt, y_sample, s0f, s0b, k1, v1, k2, v2, s3f, s3b)
```

```python
import functools
import math

import jax
import jax.numpy as jnp
from jax import lax
from jax.experimental import pallas as pl
from jax.experimental.pallas import tpu as pltpu

F32 = jnp.float32
BF16 = jnp.bfloat16

D = 2048
NB_P, L_P = 16, 256
NB_S, L_S = 2, 1024
PAST = 256
NP = NB_P * L_P
NS = NB_S * L_S
NT = NP + NS
DEPTH = 4
N_MOD = 9
D_FF = 5632
EPS = 1e-6
GRID_W = 64
ROPE_BASE = 10000.0

D_INNER = 4096
SSM_HEADS = 64
SSM_P = 64
SSM_G = 8
SSM_N = 128
CONV_CH = D_INNER + 2 * SSM_G * SSM_N
SSM_IN = D_INNER + CONV_CH + 2 * SSM_HEADS
CHUNK = 128
GW = D_INNER // SSM_G

DIFF_H = 8
DH = 128
WIN_H = 16
WIN_KV = 4
WINDOW = 128

TM = 512
VMEM_LIMIT = 48 * 1024 * 1024


def _cparams(sem):
    return pltpu.CompilerParams(dimension_semantics=sem, vmem_limit_bytes=VMEM_LIMIT)


def _silu(x):
    return x * jax.nn.sigmoid(x)


def _dot(a, b):
    return jnp.dot(a, b, preferred_element_type=F32)


def _dot_nt(a, b):
    return lax.dot_general(a, b, (((1,), (1,)), ((), ())), preferred_element_type=F32)


def _dot_tn(a, b):
    return lax.dot_general(a, b, (((0,), (0,)), ((), ())), preferred_element_type=F32)


def _seg(i, tm):
    start = i * tm
    return jnp.where(start < NP, 0, 1 + (start - NP) // L_S)


def _modnorm(x, g, shift, scale):
    y = x * lax.rsqrt(jnp.mean(x * x, axis=-1, keepdims=True) + EPS) * g
    return y * (1.0 + scale) + shift


def _mod_kernel(cv_ref, w_ref, b_ref, o_ref):
    @pl.when(pl.program_id(2) == 0)
    def _():
        o_ref[0, 0] = jnp.broadcast_to(b_ref[0, 0], (8, D))

    cv = cv_ref[...]
    o_ref[0, 0] += _dot(_silu(cv).astype(BF16), w_ref[0].astype(BF16))


def _modulation(cv8, w_ada, b_ada):
    kc = 1024
    out = pl.pallas_call(
        _mod_kernel,
        grid=(DEPTH, N_MOD, D // kc),
        in_specs=[
            pl.BlockSpec((8, kc), lambda l, j, k: (0, k)),
            pl.BlockSpec((1, kc, D), lambda l, j, k: (l, k, j)),
            pl.BlockSpec((1, 1, 1, D), lambda l, j, k: (l, j, 0, 0)),
        ],
        out_specs=pl.BlockSpec((1, 1, 8, D), lambda l, j, k: (l, j, 0, 0)),
        out_shape=jax.ShapeDtypeStruct((DEPTH, N_MOD, 8, D), F32),
        compiler_params=_cparams(("parallel", "parallel", "arbitrary")),
        name="modulation",
    )(cv8, w_ada, b_ada.reshape(DEPTH, N_MOD, 1, D))
    return out.reshape(DEPTH, N_MOD, 8, 1, D)


def _mod_spec(l, k):
    return pl.BlockSpec((1, 1, 1, 1, D), lambda i, j: (l, k, _seg(i, TM), 0, 0))


def _ffn_kernel(x_ref, sh_ref, sc_ref, gt_ref, g_ref, wg_ref, wu_ref, wo_ref, o_ref,
                h_scr, acc_scr):
    j = pl.program_id(1)

    @pl.when(j == 0)
    def _():
        h = _modnorm(x_ref[...], g_ref[0, 0], sh_ref[0, 0, 0], sc_ref[0, 0, 0])
        h_scr[...] = h.astype(BF16)
        acc_scr[...] = jnp.zeros_like(acc_scr)

    h = h_scr[...]
    gate = _dot(h, wg_ref[0].astype(BF16))
    up = _dot(h, wu_ref[0].astype(BF16))
    act = (_silu(gate) * up).astype(BF16)
    acc_scr[...] += _dot(act, wo_ref[0].astype(BF16))

    @pl.when(j == pl.num_programs(1) - 1)
    def _():
        o_ref[...] = x_ref[...] + 0.5 * gt_ref[0, 0, 0] * acc_scr[...]


def _ffn(x, mods, norm_g4, w_in, w_out, l, sub):
    tf = 256
    nf = D_FF // tf
    return pl.pallas_call(
        _ffn_kernel,
        grid=(NT // TM, nf),
        in_specs=[
            pl.BlockSpec((TM, D), lambda i, j: (i, 0)),
            _mod_spec(l, 3 * sub), _mod_spec(l, 3 * sub + 1), _mod_spec(l, 3 * sub + 2),
            pl.BlockSpec((1, 1, 1, D), lambda i, j: (l, sub, 0, 0)),
            pl.BlockSpec((1, D, tf), lambda i, j: (l, 0, j)),
            pl.BlockSpec((1, D, tf), lambda i, j: (l, 0, j + nf)),
            pl.BlockSpec((1, tf, D), lambda i, j: (l, j, 0)),
        ],
        out_specs=pl.BlockSpec((TM, D), lambda i, j: (i, 0)),
        out_shape=jax.ShapeDtypeStruct((NT, D), F32),
        scratch_shapes=[pltpu.VMEM((TM, D), BF16), pltpu.VMEM((TM, D), F32)],
        compiler_params=_cparams(("parallel", "arbitrary")),
        name="ffn",
    )(x, mods, mods, mods, norm_g4, w_in, w_in, w_out)


def _inproj_kernel(x_ref, sh_ref, sc_ref, g_ref, w_ref, o_ref, h_scr):
    @pl.when(pl.program_id(1) == 0)
    def _():
        h = _modnorm(x_ref[...], g_ref[0, 0], sh_ref[0, 0, 0], sc_ref[0, 0, 0])
        h_scr[...] = h.astype(BF16)

    o_ref[...] = _dot(h_scr[...], w_ref[0].astype(BF16))


def _inproj(x, mods, norm_g4, w, l, j_w, tn):
    n = w.shape[-1]
    return pl.pallas_call(
        _inproj_kernel,
        grid=(NT // TM, n // tn),
        in_specs=[
            pl.BlockSpec((TM, D), lambda i, j: (i, 0)),
            _mod_spec(l, 3), _mod_spec(l, 4),
            pl.BlockSpec((1, 1, 1, D), lambda i, j: (l, 1, 0, 0)),
            pl.BlockSpec((1, D, tn), lambda i, j: (j_w, 0, j)),
        ],
        out_specs=pl.BlockSpec((TM, tn), lambda i, j: (i, j)),
        out_shape=jax.ShapeDtypeStruct((NT, n), F32),
        scratch_shapes=[pltpu.VMEM((TM, D), BF16)],
        compiler_params=_cparams(("parallel", "arbitrary")),
        name="inproj",
    )(x, mods, mods, norm_g4, w)


def _outproj_kernel(*refs, norm):
    if norm:
        y_ref, ng_ref, w_ref, x_ref, gt_ref, o_ref, yn_scr = refs

        @pl.when(pl.program_id(1) == 0)
        def _():
            y = y_ref[...]
            yn = y * lax.rsqrt(jnp.mean(y * y, axis=-1, keepdims=True) + EPS) * ng_ref[...]
            yn_scr[...] = yn.astype(BF16)

        y = yn_scr[...]
    else:
        y_ref, w_ref, x_ref, gt_ref, o_ref = refs
        y = y_ref[...]
    o_ref[...] = x_ref[...] + gt_ref[0, 0, 0] * _dot(y, w_ref[0].astype(BF16))


def _outproj(y, w, j_w, x, mods, l, norm_g=None):
    tn = 512
    k = y.shape[-1]
    norm = norm_g is not None
    in_specs = [pl.BlockSpec((TM, k), lambda i, j: (i, 0))]
    args = [y]
    scratch = []
    if norm:
        in_specs.append(pl.BlockSpec((1, k), lambda i, j: (0, 0)))
        args.append(norm_g.reshape(1, k))
        scratch.append(pltpu.VMEM((TM, k), BF16))
    in_specs += [
        pl.BlockSpec((1, k, tn), lambda i, j: (j_w, 0, j)),
        pl.BlockSpec((TM, tn), lambda i, j: (i, j)),
        pl.BlockSpec((1, 1, 1, 1, tn), lambda i, j: (l, 5, _seg(i, TM), 0, j)),
    ]
    args += [w, x, mods]
    return pl.pallas_call(
        functools.partial(_outproj_kernel, norm=norm),
        grid=(NT // TM, D // tn),
        in_specs=in_specs,
        out_specs=pl.BlockSpec((TM, tn), lambda i, j: (i, j)),
        out_shape=jax.ShapeDtypeStruct((NT, D), F32),
        scratch_shapes=scratch,
        compiler_params=_cparams(("parallel", "arbitrary")),
        name="outproj",
    )(*args)


def _rmsnorm_kernel(x_ref, g_ref, o_ref):
    x = x_ref[...]
    o_ref[...] = x * lax.rsqrt(jnp.mean(x * x, axis=-1, keepdims=True) + EPS) * g_ref[...]


def _final_norm(x, g):
    return pl.pallas_call(
        _rmsnorm_kernel,
        grid=(NT // TM,),
        in_specs=[pl.BlockSpec((TM, D), lambda i: (i, 0)),
                  pl.BlockSpec((1, D), lambda i: (0, 0))],
        out_specs=pl.BlockSpec((TM, D), lambda i: (i, 0)),
        out_shape=jax.ShapeDtypeStruct((NT, D), F32),
        compiler_params=_cparams(("parallel",)),
        name="final_norm",
    )(x, g.reshape(1, D))


def _split3(x):
    hi = x.astype(BF16)
    r = x - hi.astype(F32)
    mid = r.astype(BF16)
    lo = (r - mid.astype(F32)).astype(BF16)
    return hi, mid, lo


def _lane_col(x, r):
    lane = lax.broadcasted_iota(jnp.int32, x.shape, 1)
    return jnp.sum(jnp.where(lane == r, x, 0.0), axis=1, keepdims=True)


def _softplus(x):
    return jnp.maximum(x, 0.0) + jnp.log1p(jnp.exp(-jnp.abs(x)))


def _ssd_kernel(*refs, L, has_h0, emit_state):
    (z_ref, xs_ref, bm_ref, cm_ref, dtc_ref, dtr_ref, cwx_ref, cwb_ref, cwc_ref,
     cbx_ref, cbb_ref, cbc_ref, dtbc_ref, dtbr_ref, alc_ref, alr_ref, dsk_ref) = refs[:17]
    pos = 17
    if has_h0:
        h0f_ref, h0b_ref = refs[pos:pos + 2]
        pos += 2
    y_ref = refs[pos]
    pos += 1
    if emit_state:
        hf_ref, hb_ref = refs[pos:pos + 2]
        pos += 2
    xs_s, bm_s, cm_s, dt_s, h_s = refs[pos:pos + 5]
    nc = L // CHUNK

    row = lax.broadcasted_iota(jnp.int32, (L, 1), 0)

    def conv(x_ref, w_ref, b_ref):
        x = x_ref[...]
        xm = jnp.where(row == 0, 0.0, pltpu.roll(x, 1, 0))
        xp = jnp.where(row == L - 1, 0.0, pltpu.roll(x, L - 1, 0))
        w = w_ref[...]
        return _silu(w[0:1] * xm + w[1:2] * x + w[2:3] * xp + b_ref[...])

    xs_s[...] = conv(xs_ref, cwx_ref, cbx_ref)
    bm_s[...] = conv(bm_ref, cwb_ref, cbb_ref)
    cm_s[...] = conv(cm_ref, cwc_ref, cbc_ref)
    dt_s[...] = _softplus(dtc_ref[0] + dtbc_ref[0])
    a_c = -jnp.exp(alc_ref[0])
    a_r = -jnp.exp(alr_ref[0])
    dtb_r = dtbr_ref[0]

    if has_h0:
        h_s[0] = h0f_ref[0]
        h_s[1] = h0b_ref[0]
    else:
        h_s[...] = jnp.zeros_like(h_s)

    ii = lax.broadcasted_iota(jnp.int32, (CHUNK, CHUNK), 0)
    jj = lax.broadcasted_iota(jnp.int32, (CHUNK, CHUNK), 1)
    lower = ii >= jj
    upper = ii <= jj
    lower_b = lower.astype(BF16)
    upper_b = upper.astype(BF16)
    lane_lo = jj < SSM_P
    row_lo = lax.broadcasted_iota(jnp.int32, (CHUNK, 1), 0) < SSM_P

    def pair_sel(x, ra, rb):
        return jnp.where(lane_lo, _lane_col(x, ra), _lane_col(x, rb))

    def chunk_body(c, carry, d):
        cc = c if d == 0 else nc - 1 - c
        r0 = pl.multiple_of(cc * CHUNK, CHUNK)
        xs_c = xs_s[pl.ds(r0, CHUNK), :]
        bm_b = bm_s[pl.ds(r0, CHUNK), :].astype(BF16)
        cm_b = cm_s[pl.ds(r0, CHUNK), :].astype(BF16)
        dt_c = dt_s[pl.ds(r0, CHUNK), :]
        dta_c = dt_c * a_c
        dta_r = _softplus(dtr_ref[0, cc] + dtb_r) * a_r
        m_col, m_row, mask = (lower_b, upper_b, lower) if d == 0 else (upper_b, lower_b, upper)
        acum_c = sum(_dot(m_col, p) for p in _split3(dta_c))
        acum_r = sum(_dot(p, m_row) for p in _split3(dta_r))
        last = acum_c[CHUNK - 1:CHUNK, :] if d == 0 else acum_c[0:1, :]
        dec_end = jnp.exp(last - acum_c)
        eac = jnp.exp(acum_c)
        cdec = jnp.exp(last)
        cb = _dot_nt(cm_b, bm_b)
        for pair in range(4):
            ra = d * 8 + 2 * pair
            rb = ra + 1
            lanes = slice(pair * CHUNK, (pair + 1) * CHUNK)
            ms = []
            for r in (ra, rb):
                seg = _lane_col(acum_c, r) - acum_r[r:r + 1, :]
                decay = jnp.exp(jnp.where(mask, seg, -jnp.inf))
                ms.append((cb * decay).astype(BF16))
            m2 = jnp.concatenate(ms, axis=1)
            x_p = xs_c[:, lanes]
            xdt = x_p * pair_sel(dt_c, ra, rb)
            xb = xdt.astype(BF16)
            zero = jnp.zeros_like(xb)
            xbd = jnp.concatenate([jnp.where(lane_lo, xb, zero),
                                   jnp.where(lane_lo, zero, xb)], axis=0)
            y_diag = _dot(m2, xbd)
            h_p = h_s[d, lanes, :]
            y_off = _dot_nt(cm_b, h_p.astype(BF16)) * pair_sel(eac, ra, rb)
            xdtd = (xdt * pair_sel(dec_end, ra, rb)).astype(BF16)
            s_new = _dot_tn(xdtd, bm_b)
            cd = jnp.where(row_lo, _lane_col(cdec, ra), _lane_col(cdec, rb))
            h_s[d, lanes, :] = h_p * cd + s_new
            yv = y_diag + y_off
            if d == 0:
                y_ref[pl.ds(r0, CHUNK), lanes] = yv + dsk_ref[:, lanes] * x_p
            else:
                zg = z_ref[pl.ds(r0, CHUNK), lanes]
                y_ref[pl.ds(r0, CHUNK), lanes] = (y_ref[pl.ds(r0, CHUNK), lanes] + yv) * _silu(zg)
        return carry

    lax.fori_loop(0, nc, functools.partial(chunk_body, d=0), 0)
    lax.fori_loop(0, nc, functools.partial(chunk_body, d=1), 0)

    if emit_state:
        hf_ref[0] = h_s[0]
        hb_ref[0] = h_s[1]


def _ssd_call(zx, dtc, dtr, conv_w, conv_b, dtb_c, dtb_r, al_c, al_r, dsk, *,
              L, nb, row_blk0, h0=None, emit_state=False, y_prev=None):
    nc = L // CHUNK
    has_h0 = h0 is not None
    rb = lambda b: row_blk0 + b
    xoff = D_INNER // GW
    boff = 2 * D_INNER // SSM_N
    coff = boff + SSM_G
    in_specs = [
        pl.BlockSpec((L, GW), lambda b, g: (rb(b), g)),
        pl.BlockSpec((L, GW), lambda b, g: (rb(b), xoff + g)),
        pl.BlockSpec((L, SSM_N), lambda b, g: (rb(b), boff + g)),
        pl.BlockSpec((L, SSM_N), lambda b, g: (rb(b), coff + g)),
        pl.BlockSpec((1, L, 128), lambda b, g: (g, rb(b), 0)),
        pl.BlockSpec((1, nc, 16, CHUNK), lambda b, g: (g, rb(b), 0, 0)),
        pl.BlockSpec((3, GW), lambda b, g: (0, g)),
        pl.BlockSpec((3, SSM_N), lambda b, g: (0, D_INNER // SSM_N + g)),
        pl.BlockSpec((3, SSM_N), lambda b, g: (0, D_INNER // SSM_N + SSM_G + g)),
        pl.BlockSpec((1, GW), lambda b, g: (0, g)),
        pl.BlockSpec((1, SSM_N), lambda b, g: (0, D_INNER // SSM_N + g)),
        pl.BlockSpec((1, SSM_N), lambda b, g: (0, D_INNER // SSM_N + SSM_G + g)),
        pl.BlockSpec((1, 1, 128), lambda b, g: (g, 0, 0)),
        pl.BlockSpec((1, 16, 1), lambda b, g: (g, 0, 0)),
        pl.BlockSpec((1, 1, 128), lambda b, g: (g, 0, 0)),
        pl.BlockSpec((1, 16, 1), lambda b, g: (g, 0, 0)),
        pl.BlockSpec((1, GW), lambda b, g: (0, g)),
    ]
    args = [zx, zx, zx, zx, dtc, dtr, conv_w, conv_w, conv_w, conv_b, conv_b, conv_b,
            dtb_c, dtb_r, al_c, al_r, dsk]
    st_spec = pl.BlockSpec((1, GW, SSM_N), lambda b, g: (b, g, 0))
    if has_h0:
        in_specs += [st_spec, st_spec]
        args += [h0[0], h0[1]]
    out_specs = [pl.BlockSpec((L, GW), lambda b, g: (rb(b), g))]
    out_shape = [jax.ShapeDtypeStruct((NT, D_INNER), F32)]
    if emit_state:
        out_specs += [st_spec, st_spec]
        out_shape += [jax.ShapeDtypeStruct((nb, D_INNER, SSM_N), F32)] * 2
    aliases = {}
    if y_prev is not None:
        in_specs.append(pl.BlockSpec(memory_space=pl.ANY))
        args.append(y_prev)
        aliases = {len(args) - 1: 0}

    def body(*refs):
        if y_prev is not None:
            n_in = len(args)
            refs = refs[:n_in - 1] + refs[n_in:]
        _ssd_kernel(*refs, L=L, has_h0=has_h0, emit_state=emit_state)

    return pl.pallas_call(
        body,
        grid=(nb, SSM_G),
        in_specs=in_specs,
        out_specs=out_specs,
        out_shape=out_shape,
        scratch_shapes=[pltpu.VMEM((L, GW), F32), pltpu.VMEM((L, SSM_N), F32),
                        pltpu.VMEM((L, SSM_N), F32), pltpu.VMEM((L, 128), F32),
                        pltpu.VMEM((2, GW, SSM_N), F32)],
        input_output_aliases=aliases,
        compiler_params=_cparams(("parallel", "parallel")),
        name="ssd",
    )(*args)


def _ssm_mixer(zx, h0_f, h0_b, conv_w, conv_b, dt_bias, a_log, d_skip):
    dt_raw = zx[:, D_INNER + CONV_CH:]
    dt_g = dt_raw.reshape(NT, 2, SSM_G, 8).transpose(2, 0, 1, 3).reshape(SSM_G, NT, 16)
    dtc = jnp.pad(dt_g, ((0, 0), (0, 0), (0, 128 - 16)))
    dtr = dt_g.reshape(SSM_G, NT // CHUNK, CHUNK, 16).transpose(0, 1, 3, 2)

    def per_group(p):
        return p.reshape(2, SSM_G, 8).transpose(1, 0, 2).reshape(SSM_G, 16)

    dtb, al = per_group(dt_bias), per_group(a_log)
    dtb_c = jnp.pad(dtb, ((0, 0), (0, 128 - 16))).reshape(SSM_G, 1, 128)
    al_c = jnp.pad(al, ((0, 0), (0, 128 - 16))).reshape(SSM_G, 1, 128)
    dtb_r = dtb.reshape(SSM_G, 16, 1)
    al_r = al.reshape(SSM_G, 16, 1)
    dsk = jnp.repeat(d_skip, SSM_P).reshape(1, D_INNER)
    cb2 = conv_b.reshape(1, CONV_CH)
    common = (zx, dtc, dtr, conv_w, cb2, dtb_c, dtb_r, al_c, al_r, dsk)
    y, hf, hb = _ssd_call(*common, L=L_P, nb=NB_P, row_blk0=0, emit_state=True)
    h0 = (h0_f.reshape(NB_S, D_INNER, SSM_N), h0_b.reshape(NB_S, D_INNER, SSM_N))
    (y,) = _ssd_call(*common, L=L_S, nb=NB_S, row_blk0=NP // L_S, h0=h0, y_prev=y)
    st = (NB_P, SSM_HEADS, SSM_P, SSM_N)
    return y, hf.reshape(st), hb.reshape(st)


def _rope_tables():
    rows = L_S // GRID_W
    rowp = jnp.repeat(jnp.arange(rows, dtype=F32), GRID_W)
    colp = jnp.tile(jnp.arange(GRID_W, dtype=F32), rows)
    nf = DH // 4
    inv = ROPE_BASE ** (-jnp.arange(nf, dtype=F32) / nf)
    a0, a1 = rowp[:, None] * inv, colp[:, None] * inv
    c0, c1, s0, s1 = jnp.cos(a0), jnp.cos(a1), jnp.sin(a0), jnp.sin(a1)
    zz = jnp.zeros_like(s0)
    cos = jnp.concatenate([c0, c0, c1, c1], axis=-1)
    sin_next = jnp.concatenate([-s0, zz, -s1, zz], axis=-1)
    sin_prev = jnp.concatenate([zz, s0, zz, s1], axis=-1)
    return cos, sin_next, sin_prev


def _rope(x, cos, sin_next, sin_prev):
    return x * cos + pltpu.roll(x, 96, 1) * sin_next + pltpu.roll(x, 32, 1) * sin_prev


def _diff_kernel(*refs, L, latent, lambda_init):
    q0_ref, q1_ref, k0_ref, k1_ref, v_ref = refs[:5]
    pos = 5
    if latent:
        ck0_ref, ck1_ref, cv_ref, cq_ref, snq_ref, spq_ref, ck_ref, snk_ref, spk_ref = refs[pos:pos + 9]
        pos += 9
    lam_ref, g_ref, o_ref, kbuf, vbuf = refs[pos:pos + 5]
    off = PAST if latent else 0

    @pl.when(pl.program_id(2) == 0)
    def _():
        if latent:
            kbuf[0, 0:PAST, :] = ck0_ref[0].astype(BF16)
            kbuf[1, 0:PAST, :] = ck1_ref[0].astype(BF16)
            vbuf[0:PAST, :] = cv_ref[0].astype(BF16)
        for m, k_ref in enumerate((k0_ref, k1_ref)):
            k = k_ref[...]
            if latent:
                k = _rope(k, ck_ref[...], snk_ref[...], spk_ref[...])
            kbuf[m, off:off + L, :] = k.astype(BF16)
        vbuf[off:off + L, :] = v_ref[...].astype(BF16)

    lv = lam_ref[...]
    lam = (jnp.exp(jnp.sum(lv[0:1] * lv[1:2], axis=1, keepdims=True))
           - jnp.exp(jnp.sum(lv[2:3] * lv[3:4], axis=1, keepdims=True)) + lambda_init)
    ps = []
    for m, q_ref in enumerate((q0_ref, q1_ref)):
        q = q_ref[...]
        if latent:
            q = _rope(q, cq_ref[...], snq_ref[...], spq_ref[...])
        s = _dot_nt(q.astype(BF16), kbuf[m]) * (DH ** -0.5)
        e = jnp.exp(s - jnp.max(s, axis=-1, keepdims=True))
        ps.append(e / jnp.sum(e, axis=-1, keepdims=True))
    p = (ps[0] - lam * ps[1]).astype(BF16)
    o = _dot(p, vbuf[...])
    o = o * lax.rsqrt(jnp.mean(o * o, axis=-1, keepdims=True) + EPS) * g_ref[...]
    o_ref[...] = (o * (1.0 - lambda_init)).astype(BF16)


def _diff_call(qkv, lam_p, subln_g, lambda_init, *, L, nb, row_blk0, qb, cache=None, tables=None):
    latent = cache is not None
    nq = L // qb
    kt = L + (PAST if latent else 0)
    rq = lambda b, qi: (row_blk0 + b) * nq + qi
    rs = lambda b: row_blk0 + b
    in_specs = [
        pl.BlockSpec((qb, DH), lambda b, h, qi: (rq(b, qi), h)),
        pl.BlockSpec((qb, DH), lambda b, h, qi: (rq(b, qi), DIFF_H + h)),
        pl.BlockSpec((L, DH), lambda b, h, qi: (rs(b), 2 * DIFF_H + h)),
        pl.BlockSpec((L, DH), lambda b, h, qi: (rs(b), 3 * DIFF_H + h)),
        pl.BlockSpec((L, 2 * DH), lambda b, h, qi: (rs(b), 2 * DIFF_H + h)),
    ]
    args = [qkv] * 5
    if latent:
        ck, cv = cache
        in_specs += [
            pl.BlockSpec((1, PAST, DH), lambda b, h, qi: (b, 0, h)),
            pl.BlockSpec((1, PAST, DH), lambda b, h, qi: (b, 0, DIFF_H + h)),
            pl.BlockSpec((1, PAST, 2 * DH), lambda b, h, qi: (b, 0, h)),
        ]
        args += [ck, ck, cv]
        in_specs += [pl.BlockSpec((qb, DH), lambda b, h, qi: (qi, 0))] * 3
        in_specs += [pl.BlockSpec((L, DH), lambda b, h, qi: (0, 0))] * 3
        args += list(tables) * 2
    in_specs += [pl.BlockSpec((4, DH), lambda b, h, qi: (0, 0)),
                 pl.BlockSpec((1, 2 * DH), lambda b, h, qi: (0, 0))]
    args += [lam_p, subln_g.reshape(1, 2 * DH)]
    return pl.pallas_call(
        functools.partial(_diff_kernel, L=L, latent=latent, lambda_init=lambda_init),
        grid=(nb, DIFF_H, nq),
        in_specs=in_specs,
        out_specs=pl.BlockSpec((qb, 2 * DH), lambda b, h, qi: (b * nq + qi, h)),
        out_shape=jax.ShapeDtypeStruct((nb * L, D), BF16),
        scratch_shapes=[pltpu.VMEM((2, kt, DH), BF16), pltpu.VMEM((kt, 2 * DH), BF16)],
        compiler_params=_cparams(("parallel", "parallel", "arbitrary")),
        name="diff_attn",
    )(*args)


def _win_kernel(*refs, L, latent):
    q_ref, k_ref, v_ref = refs[:3]
    pos = 3
    if latent:
        ck_ref, cv_ref, cq_ref, snq_ref, spq_ref, ckk_ref, snk_ref, spk_ref = refs[pos:pos + 8]
        pos += 8
    sink_ref, o_ref = refs[pos:pos + 2]
    pos += 2
    g = pl.program_id(1)
    n = pl.program_id(2)
    scale = DH ** -0.5
    qpk = WIN_H // WIN_KV

    if latent:
        kbuf, vbuf = refs[pos:pos + 2]
        koff = PAST + WINDOW

        @pl.when(n == 0)
        def _():
            zpad = jnp.zeros((WINDOW, DH), BF16)
            for buf, c_ref, x_ref, rope in ((kbuf, ck_ref, k_ref, True), (vbuf, cv_ref, v_ref, False)):
                buf[0:PAST, :] = c_ref[0].astype(BF16)
                buf[PAST:koff, :] = zpad
                x = x_ref[...]
                if rope:
                    x = _rope(x, ckk_ref[...], snk_ref[...], spk_ref[...])
                buf[koff:koff + L, :] = x.astype(BF16)
                buf[koff + L:koff + L + WINDOW, :] = zpad

        band0 = pl.multiple_of(PAST + n * WINDOW, WINDOW)
        kc, vc = kbuf[0:PAST, :], vbuf[0:PAST, :]
        kb, vb = kbuf[pl.ds(band0, 3 * WINDOW), :], vbuf[pl.ds(band0, 3 * WINDOW), :]
        qi_ = lax.broadcasted_iota(jnp.int32, (WINDOW, 3 * WINDOW), 0)
        kj_ = lax.broadcasted_iota(jnp.int32, (WINDOW, 3 * WINDOW), 1)
        kpos = (n - 1) * WINDOW + kj_
        delta = kj_ - qi_
        valid = (delta >= 0) & (delta <= 2 * WINDOW) & (kpos >= 0) & (kpos < L)
    else:
        kc, vc = k_ref[...].astype(BF16), v_ref[...].astype(BF16)

    for r in range(qpk):
        q = q_ref[:, r * DH:(r + 1) * DH]
        if latent:
            q = _rope(q, cq_ref[...], snq_ref[...], spq_ref[...])
        qb_ = q.astype(BF16)
        sink = sink_ref[g * qpk + r]
        s_c = _dot_nt(qb_, kc) * scale
        mx = jnp.maximum(jnp.max(s_c, axis=-1, keepdims=True), sink)
        if latent:
            s_b = jnp.where(valid, _dot_nt(qb_, kb) * scale, -jnp.inf)
            mx = jnp.maximum(mx, jnp.max(s_b, axis=-1, keepdims=True))
        e_c = jnp.exp(s_c - mx)
        den = jnp.sum(e_c, axis=-1, keepdims=True) + jnp.exp(sink - mx)
        if latent:
            e_b = jnp.exp(s_b - mx)
            den = den + jnp.sum(e_b, axis=-1, keepdims=True)
        o = _dot((e_c / den).astype(BF16), vc)
        if latent:
            o = o + _dot((e_b / den).astype(BF16), vb)
        o_ref[:, r * DH:(r + 1) * DH] = o.astype(BF16)


def _win_call(qkv, sink, *, L, nb, row_blk0, qb, cache=None, tables=None):
    latent = cache is not None
    nq = L // qb
    qw = (WIN_H // WIN_KV) * DH
    koff = WIN_H
    voff = WIN_H + WIN_KV
    rq = lambda b, qi: (row_blk0 + b) * nq + qi
    rs = lambda b: row_blk0 + b
    in_specs = [
        pl.BlockSpec((qb, qw), lambda b, g, qi: (rq(b, qi), g)),
        pl.BlockSpec((L, DH), lambda b, g, qi: (rs(b), koff + g)),
        pl.BlockSpec((L, DH), lambda b, g, qi: (rs(b), voff + g)),
    ]
    args = [qkv] * 3
    scratch = []
    if latent:
        ck, cv = cache
        in_specs += [pl.BlockSpec((1, PAST, DH), lambda b, g, qi: (b, 0, g))] * 2
        args += [ck, cv]
        in_specs += [pl.BlockSpec((qb, DH), lambda b, g, qi: (qi, 0))] * 3
        in_specs += [pl.BlockSpec((L, DH), lambda b, g, qi: (0, 0))] * 3
        args += list(tables) * 2
        kt = PAST + L + 2 * WINDOW
        scratch = [pltpu.VMEM((kt, DH), BF16), pltpu.VMEM((kt, DH), BF16)]
    in_specs.append(pl.BlockSpec(memory_space=pltpu.SMEM))
    args.append(sink)
    return pl.pallas_call(
        functools.partial(_win_kernel, L=L, latent=latent),
        grid=(nb, WIN_KV, nq),
        in_specs=in_specs,
        out_specs=pl.BlockSpec((qb, qw), lambda b, g, qi: (b * nq + qi, g)),
        out_shape=jax.ShapeDtypeStruct((nb * L, D), BF16),
        scratch_shapes=scratch,
        compiler_params=_cparams(("parallel", "parallel", "arbitrary")),
        name="win_attn",
    )(*args)


def kernel(x_prompt, x_sample, state_l0_fwd, state_l0_bwd, cache_l1_k, cache_l1_v, cache_l2_k, cache_l2_v, state_l3_fwd, state_l3_bwd, c, c_ctx, norm_g, w_ada, b_ada, ffn1_w_in, ffn1_w_out, ffn2_w_in, ffn2_w_out, ssm_w_in, ssm_conv_w, ssm_conv_b, ssm_dt_bias, ssm_a_log, ssm_d, ssm_norm_g, ssm_w_out, diff_w_qkv, diff_lambda, diff_subln_g, diff_w_out, win_w_qkv, win_sink, win_w_out, final_norm_g):
    x = jnp.concatenate([x_prompt.reshape(NP, D), x_sample.reshape(NS, D)], axis=0)
    cv8 = jnp.concatenate([c_ctx[None, :], c, jnp.zeros((8 - 1 - NB_S, D), F32)], axis=0)
    mods = _modulation(cv8, w_ada, b_ada)
    norm_g4 = norm_g.reshape(DEPTH, 3, 1, D)
    tables = _rope_tables()
    ssm_states = [(state_l0_fwd, state_l0_bwd), (state_l3_fwd, state_l3_bwd)]
    new_state = []
    for l in range(DEPTH):
        m, j = l % 3, l // 3
        x = _ffn(x, mods, norm_g4, ffn1_w_in, ffn1_w_out, l, 0)
        if m == 0:
            zx = _inproj(x, mods, norm_g4, ssm_w_in, l, j, 1152)
            y, hf, hb = _ssm_mixer(zx, *ssm_states[j], ssm_conv_w[j], ssm_conv_b[j],
                                   ssm_dt_bias[j], ssm_a_log[j], ssm_d[j])
            x = _outproj(y, ssm_w_out, j, x, mods, l, norm_g=ssm_norm_g[j])
            new_state += [hf, hb]
        elif m == 1:
            lambda_init = 0.8 - 0.6 * math.exp(-0.3 * l)
            qkv = _inproj(x, mods, norm_g4, diff_w_qkv, l, j, 512)
            o_p = _diff_call(qkv, diff_lambda[j], diff_subln_g[j], lambda_init,
                             L=L_P, nb=NB_P, row_blk0=0, qb=L_P)
            cache = (cache_l1_k.reshape(NB_S, PAST, D), cache_l1_v.reshape(NB_S, PAST, D))
            o_s = _diff_call(qkv, diff_lambda[j], diff_subln_g[j], lambda_init,
                             L=L_S, nb=NB_S, row_blk0=NP // L_S, qb=256, cache=cache, tables=tables)
            x = _outproj(jnp.concatenate([o_p, o_s], axis=0), diff_w_out, j, x, mods, l)
            new_state += [qkv[:NP, D:2 * D].reshape(NB_P, L_P, 2, DIFF_H, DH),
                          qkv[:NP, 2 * D:].reshape(NB_P, L_P, DIFF_H, 2 * DH)]
        else:
            qkv = _inproj(x, mods, norm_g4, win_w_qkv, l, j, 512)
            o_p = _win_call(qkv, win_sink[j], L=L_P, nb=NB_P, row_blk0=0, qb=L_P)
            kvw = WIN_KV * DH
            cache = (cache_l2_k.reshape(NB_S, PAST, kvw), cache_l2_v.reshape(NB_S, PAST, kvw))
            o_s = _win_call(qkv, win_sink[j], L=L_S, nb=NB_S, row_blk0=NP // L_S, qb=WINDOW,
                            cache=cache, tables=tables)
            x = _outproj(jnp.concatenate([o_p, o_s], axis=0), win_w_out, j, x, mods, l)
            new_state += [qkv[:NP, D:D + kvw].reshape(NB_P, L_P, WIN_KV, DH),
                          qkv[:NP, D + kvw:].reshape(NB_P, L_P, WIN_KV, DH)]
        x = _ffn(x, mods, norm_g4, ffn2_w_in, ffn2_w_out, l, 2)
    y = _final_norm(x, final_norm_g)
    s0f, s0b, k1, v1, k2, v2, s3f, s3b = new_state
    return (y[:NP].reshape(NB_P, L_P, D), y[NP:].reshape(NB_S, L_S, D),
            s0f, s0b, k1, v1, k2, v2, s3f, s3b)
```

```python
import functools
import math

import jax
import jax.numpy as jnp
from jax import lax
from jax.experimental import pallas as pl
from jax.experimental.pallas import tpu as pltpu

F32 = jnp.float32
BF16 = jnp.bfloat16

D = 2048
NB_P, L_P = 16, 256
NB_S, L_S = 2, 1024
PAST = 256
NP = NB_P * L_P
NS = NB_S * L_S
NT = NP + NS
DEPTH = 4
N_MOD = 9
D_FF = 5632
EPS = 1e-6
GRID_W = 64
ROPE_BASE = 10000.0

D_INNER = 4096
SSM_HEADS = 64
SSM_P = 64
SSM_G = 8
SSM_N = 128
CONV_CH = D_INNER + 2 * SSM_G * SSM_N
SSM_IN = D_INNER + CONV_CH + 2 * SSM_HEADS
CHUNK = 128
GW = D_INNER // SSM_G

DIFF_H = 8
DH = 128
WIN_H = 16
WIN_KV = 4
WINDOW = 128

TM = 1024
VMEM_LIMIT = 56 * 1024 * 1024


def _resident_spec(block_shape, index_map):
    return pl.BlockSpec(block_shape, index_map, pipeline_mode=pl.Buffered(1))


def _cparams(sem):
    return pltpu.CompilerParams(dimension_semantics=sem, vmem_limit_bytes=VMEM_LIMIT)


def _drop_ref(kernel_fn, idx):
    def body(*refs):
        return kernel_fn(*(refs[:idx] + refs[idx + 1:]))
    return body


def _silu(x):
    return x * jax.nn.sigmoid(x)


def _dot(a, b):
    return jnp.dot(a, b, preferred_element_type=F32)


def _dot_nt(a, b):
    return lax.dot_general(a, b, (((1,), (1,)), ((), ())), preferred_element_type=F32)


def _dot_tn(a, b):
    return lax.dot_general(a, b, (((0,), (0,)), ((), ())), preferred_element_type=F32)


def _seg(i, tm):
    start = i * tm
    return jnp.where(start < NP, 0, 1 + (start - NP) // L_S)


def _modnorm(x, g, shift, scale):
    y = x * lax.rsqrt(jnp.mean(x * x, axis=-1, keepdims=True) + EPS) * g
    return y * (1.0 + scale) + shift


def _mod_kernel(cv_ref, w_ref, b_ref, o_ref):
    @pl.when(pl.program_id(2) == 0)
    def _():
        o_ref[0, 0] = jnp.broadcast_to(b_ref[0, 0], (8, D))

    cv = cv_ref[...]
    o_ref[0, 0] += _dot(_silu(cv).astype(BF16), w_ref[0].astype(BF16))


def _modulation(cv8, w_ada, b_ada):
    kc = 1024
    out = pl.pallas_call(
        _mod_kernel,
        grid=(DEPTH, N_MOD, D // kc),
        in_specs=[
            pl.BlockSpec((8, kc), lambda l, j, k: (0, k)),
            pl.BlockSpec((1, kc, D), lambda l, j, k: (l, k, j)),
            pl.BlockSpec((1, 1, 1, D), lambda l, j, k: (l, j, 0, 0)),
        ],
        out_specs=pl.BlockSpec((1, 1, 8, D), lambda l, j, k: (l, j, 0, 0)),
        out_shape=jax.ShapeDtypeStruct((DEPTH, N_MOD, 8, D), F32),
        compiler_params=_cparams(("parallel", "parallel", "arbitrary")),
        name="modulation",
    )(cv8, w_ada, b_ada.reshape(DEPTH, N_MOD, 1, D))
    return out.reshape(DEPTH, N_MOD, 8, 1, D)


def _mod_spec(l, k):
    return pl.BlockSpec((1, 1, 1, 1, D), lambda i, j: (l, k, _seg(i, TM), 0, 0))


def _ffn_kernel(x_ref, sh_ref, sc_ref, gt_ref, g_ref, wg_ref, wu_ref, wo_ref, o_ref, h_scr):
    j = pl.program_id(1)

    @pl.when(j == 0)
    def _():
        h = _modnorm(x_ref[...], g_ref[0, 0], sh_ref[0, 0, 0], sc_ref[0, 0, 0])
        h_scr[...] = h.astype(BF16)
        o_ref[...] = jnp.zeros_like(o_ref)

    h = h_scr[...]
    gate = _dot(h, wg_ref[0].astype(BF16))
    up = _dot(h, wu_ref[0].astype(BF16))
    act = (_silu(gate) * up).astype(BF16)
    o_ref[...] += _dot(act, wo_ref[0].astype(BF16))

    @pl.when(j == pl.num_programs(1) - 1)
    def _():
        o_ref[...] = x_ref[...] + 0.5 * gt_ref[0, 0, 0] * o_ref[...]


def _ffn(x, mods, norm_g4, w_in, w_out, l, sub):
    tf = 256
    nf = D_FF // tf
    return pl.pallas_call(
        _ffn_kernel,
        grid=(NT // TM, nf),
        in_specs=[
            _resident_spec((TM, D), lambda i, j: (i, 0)),
            _mod_spec(l, 3 * sub), _mod_spec(l, 3 * sub + 1), _mod_spec(l, 3 * sub + 2),
            pl.BlockSpec((1, 1, 1, D), lambda i, j: (l, sub, 0, 0)),
            pl.BlockSpec((1, D, tf), lambda i, j: (l, 0, j)),
            pl.BlockSpec((1, D, tf), lambda i, j: (l, 0, j + nf)),
            pl.BlockSpec((1, tf, D), lambda i, j: (l, j, 0)),
        ],
        out_specs=pl.BlockSpec((TM, D), lambda i, j: (i, 0)),
        out_shape=jax.ShapeDtypeStruct((NT, D), F32),
        scratch_shapes=[pltpu.VMEM((TM, D), BF16)],
        compiler_params=_cparams(("parallel", "arbitrary")),
        name="ffn",
    )(x, mods, mods, mods, norm_g4, w_in, w_in, w_out)


def _inproj_kernel(x_ref, sh_ref, sc_ref, g_ref, w_ref, o_ref, h_scr):
    @pl.when(pl.program_id(1) == 0)
    def _():
        h = _modnorm(x_ref[...], g_ref[0, 0], sh_ref[0, 0, 0], sc_ref[0, 0, 0])
        h_scr[...] = h.astype(BF16)

    o_ref[...] = _dot(h_scr[...], w_ref[0].astype(BF16))


def _inproj(x, mods, norm_g4, w, l, j_w, tn):
    n = w.shape[-1]
    return pl.pallas_call(
        _inproj_kernel,
        grid=(NT // TM, n // tn),
        in_specs=[
            _resident_spec((TM, D), lambda i, j: (i, 0)),
            _mod_spec(l, 3), _mod_spec(l, 4),
            pl.BlockSpec((1, 1, 1, D), lambda i, j: (l, 1, 0, 0)),
            pl.BlockSpec((1, D, tn), lambda i, j: (j_w, 0, j)),
        ],
        out_specs=pl.BlockSpec((TM, tn), lambda i, j: (i, j)),
        out_shape=jax.ShapeDtypeStruct((NT, n), F32),
        scratch_shapes=[pltpu.VMEM((TM, D), BF16)],
        compiler_params=_cparams(("parallel", "arbitrary")),
        name="inproj",
    )(x, mods, mods, norm_g4, w)


def _outproj_kernel(*refs, norm):
    if norm:
        y_ref, ng_ref, w_ref, x_ref, gt_ref, o_ref, yn_scr = refs

        @pl.when(pl.program_id(1) == 0)
        def _():
            def rows(t, carry):
                sl = pl.ds(pl.multiple_of(t * 128, 128), 128)
                y = y_ref[sl, :]
                yn = y * lax.rsqrt(jnp.mean(y * y, axis=-1, keepdims=True) + EPS) * ng_ref[...]
                yn_scr[sl, :] = yn.astype(BF16)
                return carry

            lax.fori_loop(0, TM // 128, rows, 0)

        y = yn_scr[...]
    else:
        y_ref, w_ref, x_ref, gt_ref, o_ref = refs
        y = y_ref[...]
    o_ref[...] = x_ref[...] + gt_ref[0, 0, 0] * _dot(y, w_ref[0].astype(BF16))


def _outproj(y, w, j_w, x, mods, l, norm_g=None):
    tn = 512
    k = y.shape[-1]
    norm = norm_g is not None
    in_specs = [_resident_spec((TM, k), lambda i, j: (i, 0))]
    args = [y]
    scratch = []
    if norm:
        in_specs.append(pl.BlockSpec((1, k), lambda i, j: (0, 0)))
        args.append(norm_g.reshape(1, k))
        scratch.append(pltpu.VMEM((TM, k), BF16))
    in_specs += [
        pl.BlockSpec((1, k, tn), lambda i, j: (j_w, 0, j)),
        pl.BlockSpec((TM, tn), lambda i, j: (i, j)),
        pl.BlockSpec((1, 1, 1, 1, tn), lambda i, j: (l, 5, _seg(i, TM), 0, j)),
    ]
    args += [w, x, mods]
    return pl.pallas_call(
        functools.partial(_outproj_kernel, norm=norm),
        grid=(NT // TM, D // tn),
        in_specs=in_specs,
        out_specs=pl.BlockSpec((TM, tn), lambda i, j: (i, j)),
        out_shape=jax.ShapeDtypeStruct((NT, D), F32),
        scratch_shapes=scratch,
        compiler_params=_cparams(("parallel", "arbitrary")),
        name="outproj",
    )(*args)


def _rmsnorm_kernel(x_ref, g_ref, o_ref):
    x = x_ref[...]
    o_ref[...] = x * lax.rsqrt(jnp.mean(x * x, axis=-1, keepdims=True) + EPS) * g_ref[...]


def _final_norm(x, g, row0, nrows):
    blk0 = row0 // TM
    return pl.pallas_call(
        _rmsnorm_kernel,
        grid=(nrows // TM,),
        in_specs=[pl.BlockSpec((TM, D), lambda i: (blk0 + i, 0)),
                  pl.BlockSpec((1, D), lambda i: (0, 0))],
        out_specs=pl.BlockSpec((TM, D), lambda i: (i, 0)),
        out_shape=jax.ShapeDtypeStruct((nrows, D), F32),
        compiler_params=_cparams(("parallel",)),
        name="final_norm",
    )(x, g.reshape(1, D))


def _split3(x):
    hi = x.astype(BF16)
    r = x - hi.astype(F32)
    mid = r.astype(BF16)
    lo = (r - mid.astype(F32)).astype(BF16)
    return hi, mid, lo


def _softplus(x):
    return jnp.maximum(x, 0.0) + jnp.log1p(jnp.exp(-jnp.abs(x)))


def _ssd_kernel(*refs, L, has_h0, emit_state):
    (z_ref, xs_ref, bm_ref, cm_ref, dt_ref, dtr_ref, cwx_ref, cwb_ref, cwc_ref,
     cbx_ref, cbb_ref, cbc_ref, dtb_ref, dtbr_ref, alr_ref, ala_ref, e_ref, dsk_ref) = refs[:18]
    pos = 18
    if has_h0:
        h0f_ref, h0b_ref = refs[pos:pos + 2]
        pos += 2
    y_ref = refs[pos]
    pos += 1
    if emit_state:
        hf_ref, hb_ref = refs[pos:pos + 2]
        pos += 2
    xs_s, bm_s, cm_s, dt_s, yb_s, ht_s = refs[pos:pos + 6]
    nc = L // CHUNK
    rowc = lax.broadcasted_iota(jnp.int32, (CHUNK, 1), 0)

    def prep(t, carry):
        r0 = pl.multiple_of(t * CHUNK, CHUNK)
        rows = pl.ds(r0, CHUNK)
        before = pl.ds(pl.multiple_of(jnp.maximum(r0 - 8, 0), 8), 8)
        after = pl.ds(pl.multiple_of(jnp.minimum(r0 + CHUNK, L - 8), 8), 8)
        for x_ref, w_ref, b_ref, out_s in ((xs_ref, cwx_ref, cbx_ref, xs_s),
                                           (bm_ref, cwb_ref, cbb_ref, bm_s),
                                           (cm_ref, cwc_ref, cbc_ref, cm_s)):
            x = x_ref[rows, :]
            prev = jnp.where(t == 0, 0.0, x_ref[before, :][7:8])
            nxt = jnp.where(t == nc - 1, 0.0, x_ref[after, :][0:1])
            xm = jnp.where(rowc == 0, prev, pltpu.roll(x, 1, 0))
            xp = jnp.where(rowc == CHUNK - 1, nxt, pltpu.roll(x, CHUNK - 1, 0))
            w = w_ref[...]
            out_s[rows, :] = _silu(w[0:1] * xm + w[1:2] * x + w[2:3] * xp + b_ref[...])
        dt_s[rows, :] = _softplus(dt_ref[rows, :] + dtb_ref[...])
        return carry

    lax.fori_loop(0, nc, prep, 0)
    a_all = -jnp.exp(ala_ref[...])
    a_r = -jnp.exp(alr_ref[0])
    dtb_r = dtbr_ref[0]

    if has_h0:
        ht_s[0] = jnp.transpose(h0f_ref[0])
        ht_s[1] = jnp.transpose(h0b_ref[0])
    else:
        ht_s[...] = jnp.zeros_like(ht_s)

    ii = lax.broadcasted_iota(jnp.int32, (CHUNK, CHUNK), 0)
    jj = lax.broadcasted_iota(jnp.int32, (CHUNK, CHUNK), 1)
    lower = ii >= jj
    upper = ii <= jj
    lane_lo = jj < SSM_P
    lower_k = jnp.concatenate([lower.astype(BF16)] * 3, axis=1)
    upper_k = jnp.concatenate([upper.astype(BF16)] * 3, axis=1)
    lower_r = jnp.concatenate([lower.astype(BF16)] * 3, axis=0)
    upper_r = jnp.concatenate([upper.astype(BF16)] * 3, axis=0)

    def direction(d, c):
        cc = c if d == 0 else nc - 1 - c
        r0 = pl.multiple_of(cc * CHUNK, CHUNK)
        xs_c = xs_s[pl.ds(r0, CHUNK), :]
        bm_b = bm_s[pl.ds(r0, CHUNK), :].astype(BF16)
        cm_b = cm_s[pl.ds(r0, CHUNK), :].astype(BF16)
        dt_c = dt_s[pl.ds(r0, CHUNK), :]
        dta_r = _softplus(dtr_ref[0, cc] + dtb_r) * a_r
        m_col, m_row, mask = (lower_k, upper_r, lower) if d == 0 else (upper_k, lower_r, upper)
        acum_c = _dot(m_col, jnp.concatenate(_split3(dt_c * a_all), axis=0))
        acum_r = _dot(jnp.concatenate(_split3(dta_r), axis=1), m_row)
        e_d = e_ref[0, d]
        dt_e = _dot(jnp.concatenate(_split3(dt_c)[:2], axis=1), e_d)
        acum_e = _dot(jnp.concatenate(_split3(acum_c)[:2], axis=1), e_d)
        last_e = acum_e[CHUNK - 1:CHUNK, :] if d == 0 else acum_e[0:1, :]
        cb = _dot_nt(cm_b, bm_b)
        xdt = xs_c * dt_e
        xdtd = (xdt * jnp.exp(last_e - acum_e)).astype(BF16)
        ht = ht_s[d]
        y_off = _dot(cm_b, ht.astype(BF16)) * jnp.exp(acum_e)
        ht_s[d] = ht * jnp.exp(last_e) + _dot_tn(bm_b, xdtd)
        for pair in range(4):
            ra = d * 8 + 2 * pair
            lanes = slice(pair * CHUNK, (pair + 1) * CHUNK)
            blk = acum_e[:, lanes]
            swp = pltpu.roll(blk, SSM_P, 1)
            ms = []
            for r, col in ((ra, jnp.where(lane_lo, blk, swp)), (ra + 1, jnp.where(lane_lo, swp, blk))):
                seg = col - acum_r[r:r + 1, :]
                decay = jnp.exp(jnp.where(mask, seg, -jnp.inf))
                ms.append((cb * decay).astype(BF16))
            m2 = jnp.concatenate(ms, axis=1)
            xb = xdt[:, lanes].astype(BF16)
            zero = jnp.zeros_like(xb)
            xbd = jnp.concatenate([jnp.where(lane_lo, xb, zero),
                                   jnp.where(lane_lo, zero, xb)], axis=0)
            out = y_ref if d == 0 else yb_s
            out[pl.ds(r0, CHUNK), lanes] = _dot(m2, xbd) + y_off[:, lanes]

    def chunk_body(c, carry):
        direction(0, c)
        direction(1, c)
        return carry

    lax.fori_loop(0, nc, chunk_body, 0)

    def finish(t, carry):
        rows = pl.ds(pl.multiple_of(t * CHUNK, CHUNK), CHUNK)
        y = y_ref[rows, :] + yb_s[rows, :] + dsk_ref[...] * xs_s[rows, :]
        y_ref[rows, :] = y * _silu(z_ref[rows, :])
        return carry

    lax.fori_loop(0, nc, finish, 0)

    if emit_state:
        hf_ref[0] = jnp.transpose(ht_s[0])
        hb_ref[0] = jnp.transpose(ht_s[1])


def _ssd_call(zx, dtr, conv_w, conv_b, dtb, dtb_r, al_r, al_all, e_sel, dsk, *,
              L, nb, row_blk0, h0=None, emit_state=False, y_prev=None):
    nc = L // CHUNK
    has_h0 = h0 is not None
    rb = lambda b: row_blk0 + b
    xoff = D_INNER // GW
    boff = 2 * D_INNER // SSM_N
    coff = boff + SSM_G
    in_specs = [
        pl.BlockSpec((L, GW), lambda b, g: (rb(b), g)),
        pl.BlockSpec((L, GW), lambda b, g: (rb(b), xoff + g)),
        pl.BlockSpec((L, SSM_N), lambda b, g: (rb(b), boff + g)),
        pl.BlockSpec((L, SSM_N), lambda b, g: (rb(b), coff + g)),
        pl.BlockSpec((L, 128), lambda b, g: (rb(b), (D_INNER + CONV_CH) // 128)),
        pl.BlockSpec((1, nc, 16, CHUNK), lambda b, g: (g, rb(b), 0, 0)),
        pl.BlockSpec((3, GW), lambda b, g: (0, g)),
        pl.BlockSpec((3, SSM_N), lambda b, g: (0, D_INNER // SSM_N + g)),
        pl.BlockSpec((3, SSM_N), lambda b, g: (0, D_INNER // SSM_N + SSM_G + g)),
        pl.BlockSpec((1, GW), lambda b, g: (0, g)),
        pl.BlockSpec((1, SSM_N), lambda b, g: (0, D_INNER // SSM_N + g)),
        pl.BlockSpec((1, SSM_N), lambda b, g: (0, D_INNER // SSM_N + SSM_G + g)),
        pl.BlockSpec((1, 128), lambda b, g: (0, 0)),
        pl.BlockSpec((1, 16, 1), lambda b, g: (g, 0, 0)),
        pl.BlockSpec((1, 16, 1), lambda b, g: (g, 0, 0)),
        pl.BlockSpec((1, 128), lambda b, g: (0, 0)),
        pl.BlockSpec((1, 2, 256, GW), lambda b, g: (g, 0, 0, 0)),
        pl.BlockSpec((1, GW), lambda b, g: (0, g)),
    ]
    args = [zx, zx, zx, zx, zx, dtr, conv_w, conv_w, conv_w, conv_b, conv_b, conv_b,
            dtb, dtb_r, al_r, al_all, e_sel, dsk]
    st_spec = pl.BlockSpec((1, GW, SSM_N), lambda b, g: (b, g, 0))
    if has_h0:
        in_specs += [st_spec, st_spec]
        args += [h0[0], h0[1]]
    out_specs = [pl.BlockSpec((L, GW), lambda b, g: (rb(b), g))]
    out_shape = [jax.ShapeDtypeStruct((NT, D_INNER), F32)]
    if emit_state:
        out_specs += [st_spec, st_spec]
        out_shape += [jax.ShapeDtypeStruct((nb, D_INNER, SSM_N), F32)] * 2
    body = functools.partial(_ssd_kernel, L=L, has_h0=has_h0, emit_state=emit_state)
    aliases = {}
    if y_prev is not None:
        in_specs.append(pl.BlockSpec(memory_space=pl.ANY))
        args.append(y_prev)
        aliases = {len(args) - 1: 0}
        body = _drop_ref(body, len(args) - 1)
    return pl.pallas_call(
        body,
        grid=(nb, SSM_G),
        in_specs=in_specs,
        out_specs=out_specs,
        out_shape=out_shape,
        scratch_shapes=[pltpu.VMEM((L, GW), F32), pltpu.VMEM((L, SSM_N), F32),
                        pltpu.VMEM((L, SSM_N), F32), pltpu.VMEM((L, 128), F32),
                        pltpu.VMEM((L, GW), F32),
                        pltpu.VMEM((2, SSM_N, GW), F32)],
        input_output_aliases=aliases,
        compiler_params=_cparams(("parallel", "parallel")),
        name="ssd",
    )(*args)


def _ssm_mixer(zx, h0_f, h0_b, conv_w, conv_b, dt_bias, a_log, d_skip):
    dt_raw = zx[:, D_INNER + CONV_CH:]
    dt_g = dt_raw.reshape(NT, 2, SSM_G, 8).transpose(2, 0, 1, 3).reshape(SSM_G, NT, 16)
    dtr = dt_g.reshape(SSM_G, NT // CHUNK, CHUNK, 16).transpose(0, 1, 3, 2)

    def per_group(p):
        return p.reshape(2, SSM_G, 8).transpose(1, 0, 2).reshape(SSM_G, 16, 1)

    src = jnp.arange(2 * SSM_HEADS)[None, None, :, None]
    dst = (jnp.arange(2)[None, :, None, None] * SSM_HEADS + jnp.arange(SSM_G)[:, None, None, None] * 8
           + jnp.arange(GW)[None, None, None, :] // SSM_P)
    e_sel = (src == dst).astype(BF16)
    e_sel = jnp.concatenate([e_sel, e_sel], axis=2)
    dsk = jnp.repeat(d_skip, SSM_P).reshape(1, D_INNER)
    common = (zx, dtr, conv_w, conv_b.reshape(1, CONV_CH), dt_bias.reshape(1, 2 * SSM_HEADS),
              per_group(dt_bias), per_group(a_log), a_log.reshape(1, 2 * SSM_HEADS), e_sel, dsk)
    y, hf, hb = _ssd_call(*common, L=L_P, nb=NB_P, row_blk0=0, emit_state=True)
    h0 = (h0_f.reshape(NB_S, D_INNER, SSM_N), h0_b.reshape(NB_S, D_INNER, SSM_N))
    (y,) = _ssd_call(*common, L=L_S, nb=NB_S, row_blk0=NP // L_S, h0=h0, y_prev=y)
    st = (NB_P, SSM_HEADS, SSM_P, SSM_N)
    return y, hf.reshape(st), hb.reshape(st)


def _rope_tables():
    rows = L_S // GRID_W
    rowp = jnp.repeat(jnp.arange(rows, dtype=F32), GRID_W)
    colp = jnp.tile(jnp.arange(GRID_W, dtype=F32), rows)
    nf = DH // 4
    inv = ROPE_BASE ** (-jnp.arange(nf, dtype=F32) / nf)
    a0, a1 = rowp[:, None] * inv, colp[:, None] * inv
    c0, c1, s0, s1 = jnp.cos(a0), jnp.cos(a1), jnp.sin(a0), jnp.sin(a1)
    zz = jnp.zeros_like(s0)
    cos = jnp.concatenate([c0, c0, c1, c1], axis=-1)
    sin_next = jnp.concatenate([-s0, zz, -s1, zz], axis=-1)
    sin_prev = jnp.concatenate([zz, s0, zz, s1], axis=-1)
    return cos, sin_next, sin_prev


def _rope(x, cos, sin_next, sin_prev):
    return x * cos + pltpu.roll(x, 96, 1) * sin_next + pltpu.roll(x, 32, 1) * sin_prev


def _diff_kernel(*refs, L, latent, lambda_init):
    q0_ref, q1_ref, k0_ref, k1_ref, v_ref = refs[:5]
    pos = 5
    if latent:
        ck0_ref, ck1_ref, cv_ref, cq_ref, snq_ref, spq_ref, ck_ref, snk_ref, spk_ref = refs[pos:pos + 9]
        pos += 9
    lam_ref, g_ref, o_ref, kbuf, vbuf = refs[pos:pos + 5]
    off = PAST if latent else 0

    @pl.when(pl.program_id(2) == 0)
    def _():
        if latent:
            kbuf[0, 0:PAST, :] = ck0_ref[0].astype(BF16)
            kbuf[1, 0:PAST, :] = ck1_ref[0].astype(BF16)
            vbuf[0:PAST, :] = cv_ref[0].astype(BF16)
        for m, k_ref in enumerate((k0_ref, k1_ref)):
            k = k_ref[...]
            if latent:
                k = _rope(k, ck_ref[...], snk_ref[...], spk_ref[...])
            kbuf[m, off:off + L, :] = k.astype(BF16)
        vbuf[off:off + L, :] = v_ref[...].astype(BF16)

    lv = lam_ref[...]
    lam = (jnp.exp(jnp.sum(lv[0:1] * lv[1:2], axis=1, keepdims=True))
           - jnp.exp(jnp.sum(lv[2:3] * lv[3:4], axis=1, keepdims=True)) + lambda_init)
    ps = []
    for m, q_ref in enumerate((q0_ref, q1_ref)):
        q = q_ref[...]
        if latent:
            q = _rope(q, cq_ref[...], snq_ref[...], spq_ref[...])
        s = _dot_nt(q.astype(BF16), kbuf[m]) * (DH ** -0.5)
        e = jnp.exp(s - jnp.max(s, axis=-1, keepdims=True))
        ps.append(e / jnp.sum(e, axis=-1, keepdims=True))
    p = (ps[0] - lam * ps[1]).astype(BF16)
    o = _dot(p, vbuf[...])
    o = o * lax.rsqrt(jnp.mean(o * o, axis=-1, keepdims=True) + EPS) * g_ref[...]
    o_ref[...] = (o * (1.0 - lambda_init)).astype(BF16)


def _diff_call(qkv, lam_p, subln_g, lambda_init, *, L, nb, row_blk0, qb, cache=None, tables=None,
               o_prev=None):
    latent = cache is not None
    nq = L // qb
    kt = L + (PAST if latent else 0)
    rq = lambda b, qi: (row_blk0 + b) * nq + qi
    rs = lambda b: row_blk0 + b
    in_specs = [
        pl.BlockSpec((qb, DH), lambda b, h, qi: (rq(b, qi), h)),
        pl.BlockSpec((qb, DH), lambda b, h, qi: (rq(b, qi), DIFF_H + h)),
        pl.BlockSpec((L, DH), lambda b, h, qi: (rs(b), 2 * DIFF_H + h)),
        pl.BlockSpec((L, DH), lambda b, h, qi: (rs(b), 3 * DIFF_H + h)),
        pl.BlockSpec((L, 2 * DH), lambda b, h, qi: (rs(b), 2 * DIFF_H + h)),
    ]
    args = [qkv] * 5
    if latent:
        ck, cv = cache
        in_specs += [
            pl.BlockSpec((1, PAST, DH), lambda b, h, qi: (b, 0, h)),
            pl.BlockSpec((1, PAST, DH), lambda b, h, qi: (b, 0, DIFF_H + h)),
            pl.BlockSpec((1, PAST, 2 * DH), lambda b, h, qi: (b, 0, h)),
        ]
        args += [ck, ck, cv]
        in_specs += [pl.BlockSpec((qb, DH), lambda b, h, qi: (qi, 0))] * 3
        in_specs += [pl.BlockSpec((L, DH), lambda b, h, qi: (0, 0))] * 3
        args += list(tables) * 2
    in_specs += [pl.BlockSpec((4, DH), lambda b, h, qi: (0, 0)),
                 pl.BlockSpec((1, 2 * DH), lambda b, h, qi: (0, 0))]
    args += [lam_p, subln_g.reshape(1, 2 * DH)]
    body = functools.partial(_diff_kernel, L=L, latent=latent, lambda_init=lambda_init)
    aliases = {}
    if o_prev is not None:
        in_specs.append(pl.BlockSpec(memory_space=pl.ANY))
        args.append(o_prev)
        aliases = {len(args) - 1: 0}
        body = _drop_ref(body, len(args) - 1)
    return pl.pallas_call(
        body,
        grid=(nb, DIFF_H, nq),
        in_specs=in_specs,
        out_specs=pl.BlockSpec((qb, 2 * DH), lambda b, h, qi: (rq(b, qi), h)),
        out_shape=jax.ShapeDtypeStruct((NT, D), BF16),
        scratch_shapes=[pltpu.VMEM((2, kt, DH), BF16), pltpu.VMEM((kt, 2 * DH), BF16)],
        input_output_aliases=aliases,
        compiler_params=_cparams(("parallel", "parallel", "arbitrary")),
        name="diff_attn",
    )(*args)


def _win_kernel(*refs, L, latent):
    q_ref, k_ref, v_ref = refs[:3]
    pos = 3
    if latent:
        ck_ref, cv_ref, cq_ref, snq_ref, spq_ref, ckk_ref, snk_ref, spk_ref = refs[pos:pos + 8]
        pos += 8
    sink_ref, o_ref = refs[pos:pos + 2]
    pos += 2
    g = pl.program_id(1)
    n = pl.program_id(2)
    scale = DH ** -0.5
    qpk = WIN_H // WIN_KV

    if latent:
        kbuf, vbuf = refs[pos:pos + 2]
        koff = PAST + WINDOW

        @pl.when(n == 0)
        def _():
            zpad = jnp.zeros((WINDOW, DH), BF16)
            for buf, c_ref, x_ref, rope in ((kbuf, ck_ref, k_ref, True), (vbuf, cv_ref, v_ref, False)):
                buf[0:PAST, :] = c_ref[0].astype(BF16)
                buf[PAST:koff, :] = zpad
                x = x_ref[...]
                if rope:
                    x = _rope(x, ckk_ref[...], snk_ref[...], spk_ref[...])
                buf[koff:koff + L, :] = x.astype(BF16)
                buf[koff + L:koff + L + WINDOW, :] = zpad

        band0 = pl.multiple_of(PAST + n * WINDOW, WINDOW)
        kc, vc = kbuf[0:PAST, :], vbuf[0:PAST, :]
        kb, vb = kbuf[pl.ds(band0, 3 * WINDOW), :], vbuf[pl.ds(band0, 3 * WINDOW), :]
        qi_ = lax.broadcasted_iota(jnp.int32, (WINDOW, 3 * WINDOW), 0)
        kj_ = lax.broadcasted_iota(jnp.int32, (WINDOW, 3 * WINDOW), 1)
        kpos = (n - 1) * WINDOW + kj_
        delta = kj_ - qi_
        valid = (delta >= 0) & (delta <= 2 * WINDOW) & (kpos >= 0) & (kpos < L)
    else:
        kc, vc = k_ref[...].astype(BF16), v_ref[...].astype(BF16)

    for r in range(qpk):
        q = q_ref[:, r * DH:(r + 1) * DH]
        if latent:
            q = _rope(q, cq_ref[...], snq_ref[...], spq_ref[...])
        qb_ = q.astype(BF16)
        sink = sink_ref[g * qpk + r]
        s_c = _dot_nt(qb_, kc) * scale
        mx = jnp.maximum(jnp.max(s_c, axis=-1, keepdims=True), sink)
        if latent:
            s_b = jnp.where(valid, _dot_nt(qb_, kb) * scale, -jnp.inf)
            mx = jnp.maximum(mx, jnp.max(s_b, axis=-1, keepdims=True))
        e_c = jnp.exp(s_c - mx)
        den = jnp.sum(e_c, axis=-1, keepdims=True) + jnp.exp(sink - mx)
        if latent:
            e_b = jnp.exp(s_b - mx)
            den = den + jnp.sum(e_b, axis=-1, keepdims=True)
        o = _dot((e_c / den).astype(BF16), vc)
        if latent:
            o = o + _dot((e_b / den).astype(BF16), vb)
        o_ref[:, r * DH:(r + 1) * DH] = o.astype(BF16)


def _win_call(qkv, sink, *, L, nb, row_blk0, qb, cache=None, tables=None, o_prev=None):
    latent = cache is not None
    nq = L // qb
    qw = (WIN_H // WIN_KV) * DH
    koff = WIN_H
    voff = WIN_H + WIN_KV
    rq = lambda b, qi: (row_blk0 + b) * nq + qi
    rs = lambda b: row_blk0 + b
    in_specs = [
        pl.BlockSpec((qb, qw), lambda b, g, qi: (rq(b, qi), g)),
        pl.BlockSpec((L, DH), lambda b, g, qi: (rs(b), koff + g)),
        pl.BlockSpec((L, DH), lambda b, g, qi: (rs(b), voff + g)),
    ]
    args = [qkv] * 3
    scratch = []
    if latent:
        ck, cv = cache
        in_specs += [pl.BlockSpec((1, PAST, DH), lambda b, g, qi: (b, 0, g))] * 2
        args += [ck, cv]
        in_specs += [pl.BlockSpec((qb, DH), lambda b, g, qi: (qi, 0))] * 3
        in_specs += [pl.BlockSpec((L, DH), lambda b, g, qi: (0, 0))] * 3
        args += list(tables) * 2
        kt = PAST + L + 2 * WINDOW
        scratch = [pltpu.VMEM((kt, DH), BF16), pltpu.VMEM((kt, DH), BF16)]
    in_specs.append(pl.BlockSpec(memory_space=pltpu.SMEM))
    args.append(sink)
    body = functools.partial(_win_kernel, L=L, latent=latent)
    aliases = {}
    if o_prev is not None:
        in_specs.append(pl.BlockSpec(memory_space=pl.ANY))
        args.append(o_prev)
        aliases = {len(args) - 1: 0}
        body = _drop_ref(body, len(args) - 1)
    return pl.pallas_call(
        body,
        grid=(nb, WIN_KV, nq),
        in_specs=in_specs,
        out_specs=pl.BlockSpec((qb, qw), lambda b, g, qi: (rq(b, qi), g)),
        out_shape=jax.ShapeDtypeStruct((NT, D), BF16),
        scratch_shapes=scratch,
        input_output_aliases=aliases,
        compiler_params=_cparams(("parallel", "parallel", "arbitrary")),
        name="win_attn",
    )(*args)


def kernel(x_prompt, x_sample, state_l0_fwd, state_l0_bwd, cache_l1_k, cache_l1_v, cache_l2_k, cache_l2_v, state_l3_fwd, state_l3_bwd, c, c_ctx, norm_g, w_ada, b_ada, ffn1_w_in, ffn1_w_out, ffn2_w_in, ffn2_w_out, ssm_w_in, ssm_conv_w, ssm_conv_b, ssm_dt_bias, ssm_a_log, ssm_d, ssm_norm_g, ssm_w_out, diff_w_qkv, diff_lambda, diff_subln_g, diff_w_out, win_w_qkv, win_sink, win_w_out, final_norm_g):
    x = jnp.concatenate([x_prompt.reshape(NP, D), x_sample.reshape(NS, D)], axis=0)
    cv8 = jnp.concatenate([c_ctx[None, :], c, jnp.zeros((8 - 1 - NB_S, D), F32)], axis=0)
    mods = _modulation(cv8, w_ada, b_ada)
    norm_g4 = norm_g.reshape(DEPTH, 3, 1, D)
    tables = _rope_tables()
    ssm_states = [(state_l0_fwd, state_l0_bwd), (state_l3_fwd, state_l3_bwd)]
    new_state = []
    for l in range(DEPTH):
        m, j = l % 3, l // 3
        x = _ffn(x, mods, norm_g4, ffn1_w_in, ffn1_w_out, l, 0)
        if m == 0:
            zx = _inproj(x, mods, norm_g4, ssm_w_in, l, j, 1152)
            y, hf, hb = _ssm_mixer(zx, *ssm_states[j], ssm_conv_w[j], ssm_conv_b[j],
                                   ssm_dt_bias[j], ssm_a_log[j], ssm_d[j])
            x = _outproj(y, ssm_w_out, j, x, mods, l, norm_g=ssm_norm_g[j])
            new_state += [hf, hb]
        elif m == 1:
            lambda_init = 0.8 - 0.6 * math.exp(-0.3 * l)
            qkv = _inproj(x, mods, norm_g4, diff_w_qkv, l, j, 512)
            o = _diff_call(qkv, diff_lambda[j], diff_subln_g[j], lambda_init,
                           L=L_P, nb=NB_P, row_blk0=0, qb=L_P)
            cache = (cache_l1_k.reshape(NB_S, PAST, D), cache_l1_v.reshape(NB_S, PAST, D))
            o = _diff_call(qkv, diff_lambda[j], diff_subln_g[j], lambda_init,
                           L=L_S, nb=NB_S, row_blk0=NP // L_S, qb=256, cache=cache, tables=tables,
                           o_prev=o)
            x = _outproj(o, diff_w_out, j, x, mods, l)
            new_state += [qkv[:NP, D:2 * D].reshape(NB_P, L_P, 2, DIFF_H, DH),
                          qkv[:NP, 2 * D:].reshape(NB_P, L_P, DIFF_H, 2 * DH)]
        else:
            qkv = _inproj(x, mods, norm_g4, win_w_qkv, l, j, 512)
            o = _win_call(qkv, win_sink[j], L=L_P, nb=NB_P, row_blk0=0, qb=L_P)
            kvw = WIN_KV * DH
            cache = (cache_l2_k.reshape(NB_S, PAST, kvw), cache_l2_v.reshape(NB_S, PAST, kvw))
            o = _win_call(qkv, win_sink[j], L=L_S, nb=NB_S, row_blk0=NP // L_S, qb=WINDOW,
                          cache=cache, tables=tables, o_prev=o)
            x = _outproj(o, win_w_out, j, x, mods, l)
            new_state += [qkv[:NP, D:D + kvw].reshape(NB_P, L_P, WIN_KV, DH),
                          qkv[:NP, D + kvw:].reshape(NB_P, L_P, WIN_KV, DH)]
        x = _ffn(x, mods, norm_g4, ffn2_w_in, ffn2_w_out, l, 2)
    y_p = _final_norm(x, final_norm_g, 0, NP)
    y_s = _final_norm(x, final_norm_g, NP, NS)
    s0f, s0b, k1, v1, k2, v2, s3f, s3b = new_state
    return (y_p.reshape(NB_P, L_P, D), y_s.reshape(NB_S, L_S, D),
            s0f, s0b, k1, v1, k2, v2, s3f, s3b)
```

```python
import functools
import math

import jax
import jax.numpy as jnp
from jax import lax
from jax.experimental import pallas as pl
from jax.experimental.pallas import tpu as pltpu

F32 = jnp.float32
BF16 = jnp.bfloat16

D = 2048
NB_P, L_P = 16, 256
NB_S, L_S = 2, 1024
PAST = 256
NP = NB_P * L_P
NS = NB_S * L_S
NT = NP + NS
DEPTH = 4
N_MOD = 9
D_FF = 5632
EPS = 1e-6
GRID_W = 64
ROPE_BASE = 10000.0

D_INNER = 4096
SSM_HEADS = 64
SSM_P = 64
SSM_G = 8
SSM_N = 128
CONV_CH = D_INNER + 2 * SSM_G * SSM_N
SSM_IN = D_INNER + CONV_CH + 2 * SSM_HEADS
CHUNK = 128
GW = D_INNER // SSM_G

DIFF_H = 8
DH = 128
WIN_H = 16
WIN_KV = 4
WINDOW = 128

TM = 1024
ROW_CHUNK = 256
VMEM_LIMIT = 56 * 1024 * 1024


def _resident_spec(block_shape, index_map):
    return pl.BlockSpec(block_shape, index_map, pipeline_mode=pl.Buffered(1))


def _cparams(sem):
    return pltpu.CompilerParams(dimension_semantics=sem, vmem_limit_bytes=VMEM_LIMIT)


def _drop_ref(kernel_fn, idx):
    def body(*refs):
        return kernel_fn(*(refs[:idx] + refs[idx + 1:]))
    return body


def _silu(x):
    return x * (0.5 * jnp.tanh(0.5 * x) + 0.5)


def _dot(a, b):
    return jnp.dot(a, b, preferred_element_type=F32)


def _dot_nt(a, b):
    return lax.dot_general(a, b, (((1,), (1,)), ((), ())), preferred_element_type=F32)


def _dot_tn(a, b):
    return lax.dot_general(a, b, (((0,), (0,)), ((), ())), preferred_element_type=F32)


def _seg(i, tm):
    start = i * tm
    return jnp.where(start < NP, 0, 1 + (start - NP) // L_S)


def _modnorm(x, g, shift, scale):
    y = x * lax.rsqrt(jnp.mean(x * x, axis=-1, keepdims=True) + EPS) * g
    return y * (1.0 + scale) + shift


def _mod_kernel(cv_ref, w_ref, b_ref, o_ref):
    @pl.when(pl.program_id(2) == 0)
    def _():
        o_ref[0, 0] = jnp.broadcast_to(b_ref[0, 0], (8, D))

    cv = cv_ref[...]
    o_ref[0, 0] += _dot(_silu(cv).astype(BF16), w_ref[0].astype(BF16))


def _modulation(cv8, w_ada, b_ada):
    kc = 1024
    out = pl.pallas_call(
        _mod_kernel,
        grid=(DEPTH, N_MOD, D // kc),
        in_specs=[
            pl.BlockSpec((8, kc), lambda l, j, k: (0, k)),
            pl.BlockSpec((1, kc, D), lambda l, j, k: (l, k, j)),
            pl.BlockSpec((1, 1, 1, D), lambda l, j, k: (l, j, 0, 0)),
        ],
        out_specs=pl.BlockSpec((1, 1, 8, D), lambda l, j, k: (l, j, 0, 0)),
        out_shape=jax.ShapeDtypeStruct((DEPTH, N_MOD, 8, D), F32),
        compiler_params=_cparams(("parallel", "parallel", "arbitrary")),
        name="modulation",
    )(cv8, w_ada, b_ada.reshape(DEPTH, N_MOD, 1, D))
    return out.reshape(DEPTH, N_MOD, 8, 1, D)


def _mod_spec(l, k):
    return pl.BlockSpec((1, 1, 1, 1, D), lambda i, j: (l, k, _seg(i, TM), 0, 0))


def _ffn_kernel(x_ref, sh_ref, sc_ref, gt_ref, g_ref, wg_ref, wu_ref, wo_ref, o_ref, h_scr):
    j = pl.program_id(1)

    def weights():
        return wg_ref[0].astype(BF16), wu_ref[0].astype(BF16), wo_ref[0].astype(BF16)

    def hidden_chunk(h, wg, wu, wo):
        act = (_silu(_dot(h, wg)) * _dot(h, wu)).astype(BF16)
        return _dot(act, wo)

    @pl.when(j == 0)
    def _():
        w = weights()
        for r in range(TM // ROW_CHUNK):
            rows = pl.ds(r * ROW_CHUNK, ROW_CHUNK)
            h = _modnorm(x_ref[rows, :], g_ref[0, 0], sh_ref[0, 0, 0], sc_ref[0, 0, 0]).astype(BF16)
            h_scr[rows, :] = h
            o_ref[rows, :] = hidden_chunk(h, *w)

    @pl.when(j > 0)
    def _():
        o_ref[...] += hidden_chunk(h_scr[...], *weights())

    @pl.when(j == pl.num_programs(1) - 1)
    def _():
        o_ref[...] = x_ref[...] + 0.5 * gt_ref[0, 0, 0] * o_ref[...]


def _ffn(x, mods, norm_g4, w_in, w_out, l, sub):
    tf = 256
    nf = D_FF // tf
    return pl.pallas_call(
        _ffn_kernel,
        grid=(NT // TM, nf),
        in_specs=[
            _resident_spec((TM, D), lambda i, j: (i, 0)),
            _mod_spec(l, 3 * sub), _mod_spec(l, 3 * sub + 1), _mod_spec(l, 3 * sub + 2),
            pl.BlockSpec((1, 1, 1, D), lambda i, j: (l, sub, 0, 0)),
            pl.BlockSpec((1, D, tf), lambda i, j: (l, 0, j)),
            pl.BlockSpec((1, D, tf), lambda i, j: (l, 0, j + nf)),
            pl.BlockSpec((1, tf, D), lambda i, j: (l, j, 0)),
        ],
        out_specs=pl.BlockSpec((TM, D), lambda i, j: (i, 0)),
        out_shape=jax.ShapeDtypeStruct((NT, D), F32),
        scratch_shapes=[pltpu.VMEM((TM, D), BF16)],
        compiler_params=_cparams(("parallel", "arbitrary")),
        name="ffn",
    )(x, mods, mods, mods, norm_g4, w_in, w_in, w_out)


def _inproj_kernel(*refs, tail):
    if tail:
        x_ref, sh_ref, sc_ref, g_ref, w_ref, wt_ref, o_ref, ot_ref, h_scr = refs
    else:
        x_ref, sh_ref, sc_ref, g_ref, w_ref, o_ref, h_scr = refs
    j = pl.program_id(1)

    @pl.when(j == 0)
    def _():
        w = w_ref[0].astype(BF16)
        if tail:
            wt = wt_ref[0].astype(BF16)
        for r in range(TM // ROW_CHUNK):
            rows = pl.ds(r * ROW_CHUNK, ROW_CHUNK)
            h = _modnorm(x_ref[rows, :], g_ref[0, 0], sh_ref[0, 0, 0], sc_ref[0, 0, 0]).astype(BF16)
            h_scr[rows, :] = h
            o_ref[rows, :] = _dot(h, w)
            if tail:
                ot_ref[rows, :] = _dot(h, wt)

    @pl.when(j > 0)
    def _():
        o_ref[...] = _dot(h_scr[...], w_ref[0].astype(BF16))


def _inproj(x, mods, norm_g4, w, l, j_w, tn, n_main=None):
    n = w.shape[-1]
    n_main = n if n_main is None else n_main
    n_tail = n - n_main
    in_specs = [
        _resident_spec((TM, D), lambda i, j: (i, 0)),
        _mod_spec(l, 3), _mod_spec(l, 4),
        pl.BlockSpec((1, 1, 1, D), lambda i, j: (l, 1, 0, 0)),
        pl.BlockSpec((1, D, tn), lambda i, j: (j_w, 0, j)),
    ]
    args = [x, mods, mods, norm_g4, w]
    out_specs = [pl.BlockSpec((TM, tn), lambda i, j: (i, j))]
    out_shape = [jax.ShapeDtypeStruct((NT, n_main), F32)]
    if n_tail:
        in_specs.append(pl.BlockSpec((1, D, n_tail), lambda i, j: (j_w, 0, n_main // n_tail)))
        args.append(w)
        out_specs.append(pl.BlockSpec((TM, n_tail), lambda i, j: (i, 0)))
        out_shape.append(jax.ShapeDtypeStruct((NT, n_tail), F32))
    return pl.pallas_call(
        functools.partial(_inproj_kernel, tail=bool(n_tail)),
        grid=(NT // TM, n_main // tn),
        in_specs=in_specs,
        out_specs=out_specs,
        out_shape=out_shape,
        scratch_shapes=[pltpu.VMEM((TM, D), BF16)],
        compiler_params=_cparams(("parallel", "arbitrary")),
        name="inproj",
    )(*args)


def _outproj_kernel(*refs, norm):
    if norm:
        y_ref, ng_ref, w_ref, x_ref, gt_ref, o_ref, yn_scr = refs
        j = pl.program_id(1)

        @pl.when(j == 0)
        def _():
            w = w_ref[0].astype(BF16)
            for r in range(TM // ROW_CHUNK):
                rows = pl.ds(r * ROW_CHUNK, ROW_CHUNK)
                y = y_ref[rows, :]
                yn = y * lax.rsqrt(jnp.mean(y * y, axis=-1, keepdims=True) + EPS) * ng_ref[...]
                yn = yn.astype(BF16)
                yn_scr[rows, :] = yn
                o_ref[rows, :] = x_ref[rows, :] + gt_ref[0, 0, 0] * _dot(yn, w)

        @pl.when(j > 0)
        def _():
            o_ref[...] = x_ref[...] + gt_ref[0, 0, 0] * _dot(yn_scr[...], w_ref[0].astype(BF16))
    else:
        y_ref, w_ref, x_ref, gt_ref, o_ref = refs
        o_ref[...] = x_ref[...] + gt_ref[0, 0, 0] * _dot(y_ref[...], w_ref[0].astype(BF16))


def _outproj(y, w, j_w, x, mods, l, norm_g=None):
    tn = 512
    k = y.shape[-1]
    norm = norm_g is not None
    in_specs = [_resident_spec((TM, k), lambda i, j: (i, 0))]
    args = [y]
    scratch = []
    if norm:
        in_specs.append(pl.BlockSpec((1, k), lambda i, j: (0, 0)))
        args.append(norm_g.reshape(1, k))
        scratch.append(pltpu.VMEM((TM, k), BF16))
    in_specs += [
        pl.BlockSpec((1, k, tn), lambda i, j: (j_w, 0, j)),
        pl.BlockSpec((TM, tn), lambda i, j: (i, j)),
        pl.BlockSpec((1, 1, 1, 1, tn), lambda i, j: (l, 5, _seg(i, TM), 0, j)),
    ]
    args += [w, x, mods]
    return pl.pallas_call(
        functools.partial(_outproj_kernel, norm=norm),
        grid=(NT // TM, D // tn),
        in_specs=in_specs,
        out_specs=pl.BlockSpec((TM, tn), lambda i, j: (i, j)),
        out_shape=jax.ShapeDtypeStruct((NT, D), F32),
        scratch_shapes=scratch,
        compiler_params=_cparams(("parallel", "arbitrary")),
        name="outproj",
    )(*args)


def _rmsnorm_kernel(x_ref, g_ref, o_ref):
    x = x_ref[...]
    o_ref[...] = x * lax.rsqrt(jnp.mean(x * x, axis=-1, keepdims=True) + EPS) * g_ref[...]


def _final_norm(x, g, row0, nrows):
    blk0 = row0 // TM
    return pl.pallas_call(
        _rmsnorm_kernel,
        grid=(nrows // TM,),
        in_specs=[pl.BlockSpec((TM, D), lambda i: (blk0 + i, 0)),
                  pl.BlockSpec((1, D), lambda i: (0, 0))],
        out_specs=pl.BlockSpec((TM, D), lambda i: (i, 0)),
        out_shape=jax.ShapeDtypeStruct((nrows, D), F32),
        compiler_params=_cparams(("parallel",)),
        name="final_norm",
    )(x, g.reshape(1, D))


def _split3(x):
    hi = x.astype(BF16)
    r = x - hi.astype(F32)
    mid = r.astype(BF16)
    lo = (r - mid.astype(F32)).astype(BF16)
    return hi, mid, lo


def _softplus(x):
    return jnp.maximum(x, 0.0) + jnp.log1p(jnp.exp(-jnp.abs(x)))


def _ssd_kernel(*refs, L, has_h0, emit_state):
    (z_ref, xs_ref, bm_ref, cm_ref, dt_ref, dtr_ref, cwx_ref, cwb_ref, cwc_ref,
     cbx_ref, cbb_ref, cbc_ref, dtb_ref, dtbr_ref, alr_ref, ala_ref, e_ref, dsk_ref) = refs[:18]
    pos = 18
    if has_h0:
        h0f_ref, h0b_ref = refs[pos:pos + 2]
        pos += 2
    y_ref = refs[pos]
    pos += 1
    if emit_state:
        hf_ref, hb_ref = refs[pos:pos + 2]
        pos += 2
    xs_s, bm_s, cm_s, dt_s, yb_s, ht_s = refs[pos:pos + 6]
    nc = L // CHUNK
    rowc = lax.broadcasted_iota(jnp.int32, (CHUNK, 1), 0)

    def prep(t, carry):
        r0 = pl.multiple_of(t * CHUNK, CHUNK)
        rows = pl.ds(r0, CHUNK)
        before = pl.ds(pl.multiple_of(jnp.maximum(r0 - 8, 0), 8), 8)
        after = pl.ds(pl.multiple_of(jnp.minimum(r0 + CHUNK, L - 8), 8), 8)
        for x_ref, w_ref, b_ref, out_s in ((xs_ref, cwx_ref, cbx_ref, xs_s),
                                           (bm_ref, cwb_ref, cbb_ref, bm_s),
                                           (cm_ref, cwc_ref, cbc_ref, cm_s)):
            x = x_ref[rows, :]
            prev = jnp.where(t == 0, 0.0, x_ref[before, :][7:8])
            nxt = jnp.where(t == nc - 1, 0.0, x_ref[after, :][0:1])
            xm = jnp.where(rowc == 0, prev, pltpu.roll(x, 1, 0))
            xp = jnp.where(rowc == CHUNK - 1, nxt, pltpu.roll(x, CHUNK - 1, 0))
            w = w_ref[...]
            out_s[rows, :] = _silu(w[0:1] * xm + w[1:2] * x + w[2:3] * xp + b_ref[...])
        dt_s[rows, :] = _softplus(dt_ref[rows, :] + dtb_ref[...])
        return carry

    lax.fori_loop(0, nc, prep, 0)
    a_all = -jnp.exp(ala_ref[...])
    a_r = -jnp.exp(alr_ref[0])
    dtb_r = dtbr_ref[0]

    if has_h0:
        ht_s[0] = jnp.transpose(h0f_ref[0])
        ht_s[1] = jnp.transpose(h0b_ref[0])
    else:
        ht_s[...] = jnp.zeros_like(ht_s)

    ii = lax.broadcasted_iota(jnp.int32, (CHUNK, CHUNK), 0)
    jj = lax.broadcasted_iota(jnp.int32, (CHUNK, CHUNK), 1)
    lower = ii >= jj
    upper = ii <= jj
    lane_lo = jj < SSM_P
    lower_k = jnp.concatenate([lower.astype(BF16)] * 3, axis=1)
    upper_k = jnp.concatenate([upper.astype(BF16)] * 3, axis=1)
    lower_r = jnp.concatenate([lower.astype(BF16)] * 3, axis=0)
    upper_r = jnp.concatenate([upper.astype(BF16)] * 3, axis=0)

    def direction(d, c):
        cc = c if d == 0 else nc - 1 - c
        r0 = pl.multiple_of(cc * CHUNK, CHUNK)
        xs_c = xs_s[pl.ds(r0, CHUNK), :]
        bm_b = bm_s[pl.ds(r0, CHUNK), :].astype(BF16)
        cm_b = cm_s[pl.ds(r0, CHUNK), :].astype(BF16)
        dt_c = dt_s[pl.ds(r0, CHUNK), :]
        dta_r = _softplus(dtr_ref[0, cc] + dtb_r) * a_r
        m_col, m_row, mask = (lower_k, upper_r, lower) if d == 0 else (upper_k, lower_r, upper)
        acum_c = _dot(m_col, jnp.concatenate(_split3(dt_c * a_all), axis=0))
        acum_r = _dot(jnp.concatenate(_split3(dta_r), axis=1), m_row)
        e_d = e_ref[0, d]
        dt_e = _dot(jnp.concatenate(_split3(dt_c)[:2], axis=1), e_d)
        acum_e = _dot(jnp.concatenate(_split3(acum_c)[:2], axis=1), e_d)
        last_e = acum_e[CHUNK - 1:CHUNK, :] if d == 0 else acum_e[0:1, :]
        cb = _dot_nt(cm_b, bm_b)
        xdt = xs_c * dt_e
        xdtd = (xdt * jnp.exp(last_e - acum_e)).astype(BF16)
        ht = ht_s[d]
        y_off = _dot(cm_b, ht.astype(BF16)) * jnp.exp(acum_e)
        ht_s[d] = ht * jnp.exp(last_e) + _dot_tn(bm_b, xdtd)
        for pair in range(4):
            ra = d * 8 + 2 * pair
            lanes = slice(pair * CHUNK, (pair + 1) * CHUNK)
            blk = acum_e[:, lanes]
            swp = pltpu.roll(blk, SSM_P, 1)
            ms = []
            for r, col in ((ra, jnp.where(lane_lo, blk, swp)), (ra + 1, jnp.where(lane_lo, swp, blk))):
                seg = col - acum_r[r:r + 1, :]
                decay = jnp.exp(jnp.where(mask, seg, -jnp.inf))
                ms.append((cb * decay).astype(BF16))
            m2 = jnp.concatenate(ms, axis=1)
            xb = xdt[:, lanes].astype(BF16)
            zero = jnp.zeros_like(xb)
            xbd = jnp.concatenate([jnp.where(lane_lo, xb, zero),
                                   jnp.where(lane_lo, zero, xb)], axis=0)
            out = y_ref if d == 0 else yb_s
            out[pl.ds(r0, CHUNK), lanes] = _dot(m2, xbd) + y_off[:, lanes]

    def chunk_body(c, carry):
        direction(0, c)
        direction(1, c)
        return carry

    lax.fori_loop(0, nc, chunk_body, 0, unroll=2)

    def finish(t, carry):
        rows = pl.ds(pl.multiple_of(t * CHUNK, CHUNK), CHUNK)
        y = y_ref[rows, :] + yb_s[rows, :] + dsk_ref[...] * xs_s[rows, :]
        y_ref[rows, :] = y * _silu(z_ref[rows, :])
        return carry

    lax.fori_loop(0, nc, finish, 0)

    if emit_state:
        hf_ref[0] = jnp.transpose(ht_s[0])
        hb_ref[0] = jnp.transpose(ht_s[1])


def _ssd_call(zx, dt_raw, dtr, conv_w, conv_b, dtb, dtb_r, al_r, al_all, e_sel, dsk, *,
              L, nb, row_blk0, h0=None, emit_state=False, y_prev=None):
    nc = L // CHUNK
    has_h0 = h0 is not None
    rb = lambda b: row_blk0 + b
    xoff = D_INNER // GW
    boff = 2 * D_INNER // SSM_N
    coff = boff + SSM_G
    in_specs = [
        pl.BlockSpec((L, GW), lambda b, g: (rb(b), g)),
        pl.BlockSpec((L, GW), lambda b, g: (rb(b), xoff + g)),
        pl.BlockSpec((L, SSM_N), lambda b, g: (rb(b), boff + g)),
        pl.BlockSpec((L, SSM_N), lambda b, g: (rb(b), coff + g)),
        pl.BlockSpec((L, 128), lambda b, g: (rb(b), 0)),
        pl.BlockSpec((1, nc, 16, CHUNK), lambda b, g: (g, rb(b), 0, 0)),
        pl.BlockSpec((3, GW), lambda b, g: (0, g)),
        pl.BlockSpec((3, SSM_N), lambda b, g: (0, D_INNER // SSM_N + g)),
        pl.BlockSpec((3, SSM_N), lambda b, g: (0, D_INNER // SSM_N + SSM_G + g)),
        pl.BlockSpec((1, GW), lambda b, g: (0, g)),
        pl.BlockSpec((1, SSM_N), lambda b, g: (0, D_INNER // SSM_N + g)),
        pl.BlockSpec((1, SSM_N), lambda b, g: (0, D_INNER // SSM_N + SSM_G + g)),
        pl.BlockSpec((1, 128), lambda b, g: (0, 0)),
        pl.BlockSpec((1, 16, 1), lambda b, g: (g, 0, 0)),
        pl.BlockSpec((1, 16, 1), lambda b, g: (g, 0, 0)),
        pl.BlockSpec((1, 128), lambda b, g: (0, 0)),
        pl.BlockSpec((1, 2, 256, GW), lambda b, g: (g, 0, 0, 0)),
        pl.BlockSpec((1, GW), lambda b, g: (0, g)),
    ]
    args = [zx, zx, zx, zx, dt_raw, dtr, conv_w, conv_w, conv_w, conv_b, conv_b, conv_b,
            dtb, dtb_r, al_r, al_all, e_sel, dsk]
    st_spec = pl.BlockSpec((1, GW, SSM_N), lambda b, g: (b, g, 0))
    if has_h0:
        in_specs += [st_spec, st_spec]
        args += [h0[0], h0[1]]
    out_specs = [pl.BlockSpec((L, GW), lambda b, g: (rb(b), g))]
    out_shape = [jax.ShapeDtypeStruct((NT, D_INNER), F32)]
    if emit_state:
        out_specs += [st_spec, st_spec]
        out_shape += [jax.ShapeDtypeStruct((nb, D_INNER, SSM_N), F32)] * 2
    body = functools.partial(_ssd_kernel, L=L, has_h0=has_h0, emit_state=emit_state)
    aliases = {}
    if y_prev is not None:
        in_specs.append(pl.BlockSpec(memory_space=pl.ANY))
        args.append(y_prev)
        aliases = {len(args) - 1: 0}
        body = _drop_ref(body, len(args) - 1)
    return pl.pallas_call(
        body,
        grid=(nb, SSM_G),
        in_specs=in_specs,
        out_specs=out_specs,
        out_shape=out_shape,
        scratch_shapes=[pltpu.VMEM((L, GW), F32), pltpu.VMEM((L, SSM_N), F32),
                        pltpu.VMEM((L, SSM_N), F32), pltpu.VMEM((L, 128), F32),
                        pltpu.VMEM((L, GW), F32),
                        pltpu.VMEM((2, SSM_N, GW), F32)],
        input_output_aliases=aliases,
        compiler_params=_cparams(("parallel", "parallel")),
        name="ssd",
    )(*args)


def _ssm_mixer(zx, dt_raw, h0_f, h0_b, conv_w, conv_b, dt_bias, a_log, d_skip):
    dt_g = dt_raw.reshape(NT, 2, SSM_G, 8).transpose(2, 0, 1, 3).reshape(SSM_G, NT, 16)
    dtr = dt_g.reshape(SSM_G, NT // CHUNK, CHUNK, 16).transpose(0, 1, 3, 2)

    def per_group(p):
        return p.reshape(2, SSM_G, 8).transpose(1, 0, 2).reshape(SSM_G, 16, 1)

    src = jnp.arange(2 * SSM_HEADS)[None, None, :, None]
    dst = (jnp.arange(2)[None, :, None, None] * SSM_HEADS + jnp.arange(SSM_G)[:, None, None, None] * 8
           + jnp.arange(GW)[None, None, None, :] // SSM_P)
    e_sel = (src == dst).astype(BF16)
    e_sel = jnp.concatenate([e_sel, e_sel], axis=2)
    dsk = jnp.repeat(d_skip, SSM_P).reshape(1, D_INNER)
    common = (zx, dt_raw, dtr, conv_w, conv_b.reshape(1, CONV_CH), dt_bias.reshape(1, 2 * SSM_HEADS),
              per_group(dt_bias), per_group(a_log), a_log.reshape(1, 2 * SSM_HEADS), e_sel, dsk)
    y, hf, hb = _ssd_call(*common, L=L_P, nb=NB_P, row_blk0=0, emit_state=True)
    h0 = (h0_f.reshape(NB_S, D_INNER, SSM_N), h0_b.reshape(NB_S, D_INNER, SSM_N))
    (y,) = _ssd_call(*common, L=L_S, nb=NB_S, row_blk0=NP // L_S, h0=h0, y_prev=y)
    st = (NB_P, SSM_HEADS, SSM_P, SSM_N)
    return y, hf.reshape(st), hb.reshape(st)


def _rope_tables():
    rows = L_S // GRID_W
    rowp = jnp.repeat(jnp.arange(rows, dtype=F32), GRID_W)
    colp = jnp.tile(jnp.arange(GRID_W, dtype=F32), rows)
    nf = DH // 4
    inv = ROPE_BASE ** (-jnp.arange(nf, dtype=F32) / nf)
    a0, a1 = rowp[:, None] * inv, colp[:, None] * inv
    c0, c1, s0, s1 = jnp.cos(a0), jnp.cos(a1), jnp.sin(a0), jnp.sin(a1)
    zz = jnp.zeros_like(s0)
    cos = jnp.concatenate([c0, c0, c1, c1], axis=-1)
    sin_next = jnp.concatenate([-s0, zz, -s1, zz], axis=-1)
    sin_prev = jnp.concatenate([zz, s0, zz, s1], axis=-1)
    return cos, sin_next, sin_prev


def _rope(x, cos, sin_next, sin_prev):
    return x * cos + pltpu.roll(x, 96, 1) * sin_next + pltpu.roll(x, 32, 1) * sin_prev


def _diff_kernel(*refs, L, latent, lambda_init):
    q0_ref, q1_ref, k0_ref, k1_ref, v_ref = refs[:5]
    pos = 5
    if latent:
        ck0_ref, ck1_ref, cv_ref, cq_ref, snq_ref, spq_ref, ck_ref, snk_ref, spk_ref = refs[pos:pos + 9]
        pos += 9
    lam_ref, g_ref, o_ref, kbuf, vbuf = refs[pos:pos + 5]
    off = PAST if latent else 0

    @pl.when(pl.program_id(2) == 0)
    def _():
        if latent:
            kbuf[0, 0:PAST, :] = ck0_ref[0].astype(BF16)
            kbuf[1, 0:PAST, :] = ck1_ref[0].astype(BF16)
            vbuf[0:PAST, :] = cv_ref[0].astype(BF16)
        for m, k_ref in enumerate((k0_ref, k1_ref)):
            k = k_ref[...]
            if latent:
                k = _rope(k, ck_ref[...], snk_ref[...], spk_ref[...])
            kbuf[m, off:off + L, :] = k.astype(BF16)
        vbuf[off:off + L, :] = v_ref[...].astype(BF16)

    lv = lam_ref[...]
    lam = (jnp.exp(jnp.sum(lv[0:1] * lv[1:2], axis=1, keepdims=True))
           - jnp.exp(jnp.sum(lv[2:3] * lv[3:4], axis=1, keepdims=True)) + lambda_init)
    ps = []
    for m, q_ref in enumerate((q0_ref, q1_ref)):
        q = q_ref[...]
        if latent:
            q = _rope(q, cq_ref[...], snq_ref[...], spq_ref[...])
        s = _dot_nt(q.astype(BF16), kbuf[m]) * (DH ** -0.5)
        e = jnp.exp(s - jnp.max(s, axis=-1, keepdims=True))
        ps.append(e / jnp.sum(e, axis=-1, keepdims=True))
    p = (ps[0] - lam * ps[1]).astype(BF16)
    o = _dot(p, vbuf[...])
    o = o * lax.rsqrt(jnp.mean(o * o, axis=-1, keepdims=True) + EPS) * g_ref[...]
    o_ref[...] = (o * (1.0 - lambda_init)).astype(BF16)


def _diff_call(qkv, lam_p, subln_g, lambda_init, *, L, nb, row_blk0, qb, cache=None, tables=None,
               o_prev=None):
    latent = cache is not None
    nq = L // qb
    kt = L + (PAST if latent else 0)
    rq = lambda b, qi: (row_blk0 + b) * nq + qi
    rs = lambda b: row_blk0 + b
    in_specs = [
        pl.BlockSpec((qb, DH), lambda b, h, qi: (rq(b, qi), h)),
        pl.BlockSpec((qb, DH), lambda b, h, qi: (rq(b, qi), DIFF_H + h)),
        pl.BlockSpec((L, DH), lambda b, h, qi: (rs(b), 2 * DIFF_H + h)),
        pl.BlockSpec((L, DH), lambda b, h, qi: (rs(b), 3 * DIFF_H + h)),
        pl.BlockSpec((L, 2 * DH), lambda b, h, qi: (rs(b), 2 * DIFF_H + h)),
    ]
    args = [qkv] * 5
    if latent:
        ck, cv = cache
        in_specs += [
            pl.BlockSpec((1, PAST, DH), lambda b, h, qi: (b, 0, h)),
            pl.BlockSpec((1, PAST, DH), lambda b, h, qi: (b, 0, DIFF_H + h)),
            pl.BlockSpec((1, PAST, 2 * DH), lambda b, h, qi: (b, 0, h)),
        ]
        args += [ck, ck, cv]
        in_specs += [pl.BlockSpec((qb, DH), lambda b, h, qi: (qi, 0))] * 3
        in_specs += [pl.BlockSpec((L, DH), lambda b, h, qi: (0, 0))] * 3
        args += list(tables) * 2
    in_specs += [pl.BlockSpec((4, DH), lambda b, h, qi: (0, 0)),
                 pl.BlockSpec((1, 2 * DH), lambda b, h, qi: (0, 0))]
    args += [lam_p, subln_g.reshape(1, 2 * DH)]
    body = functools.partial(_diff_kernel, L=L, latent=latent, lambda_init=lambda_init)
    aliases = {}
    if o_prev is not None:
        in_specs.append(pl.BlockSpec(memory_space=pl.ANY))
        args.append(o_prev)
        aliases = {len(args) - 1: 0}
        body = _drop_ref(body, len(args) - 1)
    return pl.pallas_call(
        body,
        grid=(nb, DIFF_H, nq),
        in_specs=in_specs,
        out_specs=pl.BlockSpec((qb, 2 * DH), lambda b, h, qi: (rq(b, qi), h)),
        out_shape=jax.ShapeDtypeStruct((NT, D), BF16),
        scratch_shapes=[pltpu.VMEM((2, kt, DH), BF16), pltpu.VMEM((kt, 2 * DH), BF16)],
        input_output_aliases=aliases,
        compiler_params=_cparams(("parallel", "parallel", "arbitrary")),
        name="diff_attn",
    )(*args)


def _win_kernel(*refs, L, latent):
    q_ref, k_ref, v_ref = refs[:3]
    pos = 3
    if latent:
        ck_ref, cv_ref, cq_ref, snq_ref, spq_ref, ckk_ref, snk_ref, spk_ref = refs[pos:pos + 8]
        pos += 8
    sink_ref, o_ref = refs[pos:pos + 2]
    pos += 2
    g = pl.program_id(1)
    n = pl.program_id(2)
    scale = DH ** -0.5
    qpk = WIN_H // WIN_KV

    if latent:
        kbuf, vbuf = refs[pos:pos + 2]
        koff = PAST + WINDOW

        @pl.when(n == 0)
        def _():
            zpad = jnp.zeros((WINDOW, DH), BF16)
            for buf, c_ref, x_ref, rope in ((kbuf, ck_ref, k_ref, True), (vbuf, cv_ref, v_ref, False)):
                buf[0:PAST, :] = c_ref[0].astype(BF16)
                buf[PAST:koff, :] = zpad
                x = x_ref[...]
                if rope:
                    x = _rope(x, ckk_ref[...], snk_ref[...], spk_ref[...])
                buf[koff:koff + L, :] = x.astype(BF16)
                buf[koff + L:koff + L + WINDOW, :] = zpad

        band0 = pl.multiple_of(PAST + n * WINDOW, WINDOW)
        kc, vc = kbuf[0:PAST, :], vbuf[0:PAST, :]
        kb, vb = kbuf[pl.ds(band0, 3 * WINDOW), :], vbuf[pl.ds(band0, 3 * WINDOW), :]
        qi_ = lax.broadcasted_iota(jnp.int32, (WINDOW, 3 * WINDOW), 0)
        kj_ = lax.broadcasted_iota(jnp.int32, (WINDOW, 3 * WINDOW), 1)
        kpos = (n - 1) * WINDOW + kj_
        delta = kj_ - qi_
        valid = (delta >= 0) & (delta <= 2 * WINDOW) & (kpos >= 0) & (kpos < L)
    else:
        kc, vc = k_ref[...].astype(BF16), v_ref[...].astype(BF16)

    for r in range(qpk):
        q = q_ref[:, r * DH:(r + 1) * DH]
        if latent:
            q = _rope(q, cq_ref[...], snq_ref[...], spq_ref[...])
        qb_ = q.astype(BF16)
        sink = sink_ref[g * qpk + r]
        s_c = _dot_nt(qb_, kc) * scale
        mx = jnp.maximum(jnp.max(s_c, axis=-1, keepdims=True), sink)
        if latent:
            s_b = jnp.where(valid, _dot_nt(qb_, kb) * scale, -jnp.inf)
            mx = jnp.maximum(mx, jnp.max(s_b, axis=-1, keepdims=True))
        e_c = jnp.exp(s_c - mx)
        den = jnp.sum(e_c, axis=-1, keepdims=True) + jnp.exp(sink - mx)
        if latent:
            e_b = jnp.exp(s_b - mx)
            den = den + jnp.sum(e_b, axis=-1, keepdims=True)
        o = _dot((e_c / den).astype(BF16), vc)
        if latent:
            o = o + _dot((e_b / den).astype(BF16), vb)
        o_ref[:, r * DH:(r + 1) * DH] = o.astype(BF16)


def _win_call(qkv, sink, *, L, nb, row_blk0, qb, cache=None, tables=None, o_prev=None):
    latent = cache is not None
    nq = L // qb
    qw = (WIN_H // WIN_KV) * DH
    koff = WIN_H
    voff = WIN_H + WIN_KV
    rq = lambda b, qi: (row_blk0 + b) * nq + qi
    rs = lambda b: row_blk0 + b
    in_specs = [
        pl.BlockSpec((qb, qw), lambda b, g, qi: (rq(b, qi), g)),
        pl.BlockSpec((L, DH), lambda b, g, qi: (rs(b), koff + g)),
        pl.BlockSpec((L, DH), lambda b, g, qi: (rs(b), voff + g)),
    ]
    args = [qkv] * 3
    scratch = []
    if latent:
        ck, cv = cache
        in_specs += [pl.BlockSpec((1, PAST, DH), lambda b, g, qi: (b, 0, g))] * 2
        args += [ck, cv]
        in_specs += [pl.BlockSpec((qb, DH), lambda b, g, qi: (qi, 0))] * 3
        in_specs += [pl.BlockSpec((L, DH), lambda b, g, qi: (0, 0))] * 3
        args += list(tables) * 2
        kt = PAST + L + 2 * WINDOW
        scratch = [pltpu.VMEM((kt, DH), BF16), pltpu.VMEM((kt, DH), BF16)]
    in_specs.append(pl.BlockSpec(memory_space=pltpu.SMEM))
    args.append(sink)
    body = functools.partial(_win_kernel, L=L, latent=latent)
    aliases = {}
    if o_prev is not None:
        in_specs.append(pl.BlockSpec(memory_space=pl.ANY))
        args.append(o_prev)
        aliases = {len(args) - 1: 0}
        body = _drop_ref(body, len(args) - 1)
    return pl.pallas_call(
        body,
        grid=(nb, WIN_KV, nq),
        in_specs=in_specs,
        out_specs=pl.BlockSpec((qb, qw), lambda b, g, qi: (rq(b, qi), g)),
        out_shape=jax.ShapeDtypeStruct((NT, D), BF16),
        scratch_shapes=scratch,
        input_output_aliases=aliases,
        compiler_params=_cparams(("parallel", "parallel", "arbitrary")),
        name="win_attn",
    )(*args)


def kernel(x_prompt, x_sample, state_l0_fwd, state_l0_bwd, cache_l1_k, cache_l1_v, cache_l2_k, cache_l2_v, state_l3_fwd, state_l3_bwd, c, c_ctx, norm_g, w_ada, b_ada, ffn1_w_in, ffn1_w_out, ffn2_w_in, ffn2_w_out, ssm_w_in, ssm_conv_w, ssm_conv_b, ssm_dt_bias, ssm_a_log, ssm_d, ssm_norm_g, ssm_w_out, diff_w_qkv, diff_lambda, diff_subln_g, diff_w_out, win_w_qkv, win_sink, win_w_out, final_norm_g):
    x = jnp.concatenate([x_prompt.reshape(NP, D), x_sample.reshape(NS, D)], axis=0)
    cv8 = jnp.concatenate([c_ctx[None, :], c, jnp.zeros((8 - 1 - NB_S, D), F32)], axis=0)
    mods = _modulation(cv8, w_ada, b_ada)
    norm_g4 = norm_g.reshape(DEPTH, 3, 1, D)
    tables = _rope_tables()
    ssm_states = [(state_l0_fwd, state_l0_bwd), (state_l3_fwd, state_l3_bwd)]
    new_state = []
    for l in range(DEPTH):
        m, j = l % 3, l // 3
        x = _ffn(x, mods, norm_g4, ffn1_w_in, ffn1_w_out, l, 0)
        if m == 0:
            zx, dt_raw = _inproj(x, mods, norm_g4, ssm_w_in, l, j, 1024, n_main=D_INNER + CONV_CH)
            y, hf, hb = _ssm_mixer(zx, dt_raw, *ssm_states[j], ssm_conv_w[j], ssm_conv_b[j],
                                   ssm_dt_bias[j], ssm_a_log[j], ssm_d[j])
            x = _outproj(y, ssm_w_out, j, x, mods, l, norm_g=ssm_norm_g[j])
            new_state += [hf, hb]
        elif m == 1:
            lambda_init = 0.8 - 0.6 * math.exp(-0.3 * l)
            (qkv,) = _inproj(x, mods, norm_g4, diff_w_qkv, l, j, 512)
            o = _diff_call(qkv, diff_lambda[j], diff_subln_g[j], lambda_init,
                           L=L_P, nb=NB_P, row_blk0=0, qb=L_P)
            cache = (cache_l1_k.reshape(NB_S, PAST, D), cache_l1_v.reshape(NB_S, PAST, D))
            o = _diff_call(qkv, diff_lambda[j], diff_subln_g[j], lambda_init,
                           L=L_S, nb=NB_S, row_blk0=NP // L_S, qb=256, cache=cache, tables=tables,
                           o_prev=o)
            x = _outproj(o, diff_w_out, j, x, mods, l)
            new_state += [qkv[:NP, D:2 * D].reshape(NB_P, L_P, 2, DIFF_H, DH),
                          qkv[:NP, 2 * D:].reshape(NB_P, L_P, DIFF_H, 2 * DH)]
        else:
            (qkv,) = _inproj(x, mods, norm_g4, win_w_qkv, l, j, 512)
            o = _win_call(qkv, win_sink[j], L=L_P, nb=NB_P, row_blk0=0, qb=L_P)
            kvw = WIN_KV * DH
            cache = (cache_l2_k.reshape(NB_S, PAST, kvw), cache_l2_v.reshape(NB_S, PAST, kvw))
            o = _win_call(qkv, win_sink[j], L=L_S, nb=NB_S, row_blk0=NP // L_S, qb=WINDOW,
                          cache=cache, tables=tables, o_prev=o)
            x = _outproj(o, win_w_out, j, x, mods, l)
            new_state += [qkv[:NP, D:D + kvw].reshape(NB_P, L_P, WIN_KV, DH),
                          qkv[:NP, D + kvw:].reshape(NB_P, L_P, WIN_KV, DH)]
        x = _ffn(x, mods, norm_g4, ffn2_w_in, ffn2_w_out, l, 2)
    y_p = _final_norm(x, final_norm_g, 0, NP)
    y_s = _final_norm(x, final_norm_g, NP, NS)
    s0f, s0b, k1, v1, k2, v2, s3f, s3b = new_state
    return (y_p.reshape(NB_P, L_P, D), y_s.reshape(NB_S, L_S, D),
            s0f, s0b, k1, v1, k2, v2, s3f, s3b)
```

```python
import functools
import math

import jax
import jax.numpy as jnp
from jax import lax
from jax.experimental import pallas as pl
from jax.experimental.pallas import tpu as pltpu

F32 = jnp.float32
BF16 = jnp.bfloat16

D = 2048
NB_P, L_P = 16, 256
NB_S, L_S = 2, 1024
PAST = 256
NP = NB_P * L_P
NS = NB_S * L_S
NT = NP + NS
DEPTH = 4
N_MOD = 9
D_FF = 5632
EPS = 1e-6
GRID_W = 64
ROPE_BASE = 10000.0

D_INNER = 4096
SSM_HEADS = 64
SSM_P = 64
SSM_G = 8
SSM_N = 128
CONV_CH = D_INNER + 2 * SSM_G * SSM_N
SSM_IN = D_INNER + CONV_CH + 2 * SSM_HEADS
CHUNK = 128
GW = D_INNER // SSM_G

DIFF_H = 8
DH = 128
WIN_H = 16
WIN_KV = 4
WINDOW = 128

TM = 1024
ROW_CHUNK = 256
STACK_ROWS = 512
VMEM_LIMIT = 60 * 1024 * 1024


def _resident_spec(block_shape, index_map):
    return pl.BlockSpec(block_shape, index_map, pipeline_mode=pl.Buffered(1))


def _cparams(sem):
    return pltpu.CompilerParams(dimension_semantics=sem, vmem_limit_bytes=VMEM_LIMIT)


def _drop_ref(kernel_fn, idx):
    def body(*refs):
        return kernel_fn(*(refs[:idx] + refs[idx + 1:]))
    return body


def _silu(x):
    return x * (0.5 * jnp.tanh(0.5 * x) + 0.5)


def _dot(a, b):
    return jnp.dot(a, b, preferred_element_type=F32)


def _dot_nt(a, b):
    return lax.dot_general(a, b, (((1,), (1,)), ((), ())), preferred_element_type=F32)


def _dot_tn(a, b):
    return lax.dot_general(a, b, (((0,), (0,)), ((), ())), preferred_element_type=F32)


def _seg(i, tm):
    start = i * tm
    return jnp.where(start < NP, 0, 1 + (start - NP) // L_S)


def _modnorm(x, g, shift, scale):
    y = x * lax.rsqrt(jnp.mean(x * x, axis=-1, keepdims=True) + EPS) * g
    return y * (1.0 + scale) + shift


def _stack_kernel(xp_ref, xs_ref, o_ref):
    i = pl.program_id(0)

    @pl.when(i < NP // STACK_ROWS)
    def _():
        o_ref[...] = xp_ref[...]

    @pl.when(i >= NP // STACK_ROWS)
    def _():
        o_ref[...] = xs_ref[...]


def _stack_tokens(xp, xs):
    npb = NP // STACK_ROWS
    return pl.pallas_call(
        _stack_kernel,
        grid=(NT // STACK_ROWS,),
        in_specs=[pl.BlockSpec((STACK_ROWS, D), lambda i: (jnp.minimum(i, npb - 1), 0)),
                  pl.BlockSpec((STACK_ROWS, D), lambda i: (jnp.maximum(i - npb, 0), 0))],
        out_specs=pl.BlockSpec((STACK_ROWS, D), lambda i: (i, 0)),
        out_shape=jax.ShapeDtypeStruct((NT, D), F32),
        compiler_params=_cparams(("arbitrary",)),
        name="stack_tokens",
    )(xp, xs)


def _mod_kernel(cv_ref, w_ref, b_ref, o_ref):
    @pl.when(pl.program_id(2) == 0)
    def _():
        o_ref[0, 0] = jnp.broadcast_to(b_ref[0, 0], (8, D))

    cv = cv_ref[...]
    o_ref[0, 0] += _dot(_silu(cv).astype(BF16), w_ref[0].astype(BF16))


def _modulation(cv8, w_ada, b_ada):
    kc = 1024
    out = pl.pallas_call(
        _mod_kernel,
        grid=(DEPTH, N_MOD, D // kc),
        in_specs=[
            pl.BlockSpec((8, kc), lambda l, j, k: (0, k)),
            pl.BlockSpec((1, kc, D), lambda l, j, k: (l, k, j)),
            pl.BlockSpec((1, 1, 1, D), lambda l, j, k: (l, j, 0, 0)),
        ],
        out_specs=pl.BlockSpec((1, 1, 8, D), lambda l, j, k: (l, j, 0, 0)),
        out_shape=jax.ShapeDtypeStruct((DEPTH, N_MOD, 8, D), F32),
        compiler_params=_cparams(("parallel", "parallel", "arbitrary")),
        name="modulation",
    )(cv8, w_ada, b_ada.reshape(DEPTH, N_MOD, 1, D))
    return out.reshape(DEPTH, N_MOD, 8, 1, D)


def _mod_spec(l, k):
    return pl.BlockSpec((1, 1, 1, 1, D), lambda i, j: (l, k, _seg(i, TM), 0, 0))


def _ffn_kernel(x_ref, sh_ref, sc_ref, gt_ref, g_ref, wg_ref, wu_ref, wo_ref, o_ref, h_scr):
    j = pl.program_id(1)

    def weights():
        return wg_ref[0].astype(BF16), wu_ref[0].astype(BF16), wo_ref[0].astype(BF16)

    def hidden_chunk(h, wg, wu, wo):
        act = (_silu(_dot(h, wg)) * _dot(h, wu)).astype(BF16)
        return _dot(act, wo)

    @pl.when(j == 0)
    def _():
        w = weights()
        for r in range(TM // ROW_CHUNK):
            rows = pl.ds(r * ROW_CHUNK, ROW_CHUNK)
            h = _modnorm(x_ref[rows, :], g_ref[0, 0], sh_ref[0, 0, 0], sc_ref[0, 0, 0]).astype(BF16)
            h_scr[rows, :] = h
            o_ref[rows, :] = hidden_chunk(h, *w)

    @pl.when(j > 0)
    def _():
        o_ref[...] += hidden_chunk(h_scr[...], *weights())

    @pl.when(j == pl.num_programs(1) - 1)
    def _():
        o_ref[...] = x_ref[...] + 0.5 * gt_ref[0, 0, 0] * o_ref[...]


def _ffn(x, mods, norm_g4, w_in, w_out, l, sub):
    tf = 256
    nf = D_FF // tf
    return pl.pallas_call(
        _ffn_kernel,
        grid=(NT // TM, nf),
        in_specs=[
            pl.BlockSpec((TM, D), lambda i, j: (i, 0)),
            _mod_spec(l, 3 * sub), _mod_spec(l, 3 * sub + 1), _mod_spec(l, 3 * sub + 2),
            pl.BlockSpec((1, 1, 1, D), lambda i, j: (l, sub, 0, 0)),
            pl.BlockSpec((1, D, tf), lambda i, j: (l, 0, j)),
            pl.BlockSpec((1, D, tf), lambda i, j: (l, 0, j + nf)),
            pl.BlockSpec((1, tf, D), lambda i, j: (l, j, 0)),
        ],
        out_specs=pl.BlockSpec((TM, D), lambda i, j: (i, 0)),
        out_shape=jax.ShapeDtypeStruct((NT, D), F32),
        scratch_shapes=[pltpu.VMEM((TM, D), BF16)],
        compiler_params=_cparams(("parallel", "arbitrary")),
        name="ffn",
    )(x, mods, mods, mods, norm_g4, w_in, w_in, w_out)


def _inproj_kernel(*refs, tail):
    if tail:
        x_ref, sh_ref, sc_ref, g_ref, w_ref, wt_ref, o_ref, ot_ref, h_scr = refs
    else:
        x_ref, sh_ref, sc_ref, g_ref, w_ref, o_ref, h_scr = refs
    j = pl.program_id(1)

    @pl.when(j == 0)
    def _():
        w = w_ref[0].astype(BF16)
        if tail:
            wt = wt_ref[0].astype(BF16)
        for r in range(TM // ROW_CHUNK):
            rows = pl.ds(r * ROW_CHUNK, ROW_CHUNK)
            h = _modnorm(x_ref[rows, :], g_ref[0, 0], sh_ref[0, 0, 0], sc_ref[0, 0, 0]).astype(BF16)
            h_scr[rows, :] = h
            o_ref[rows, :] = _dot(h, w)
            if tail:
                ot_ref[rows, :] = _dot(h, wt)

    @pl.when(j > 0)
    def _():
        o_ref[...] = _dot(h_scr[...], w_ref[0].astype(BF16))


def _inproj(x, mods, norm_g4, w, l, j_w, tn, n_main=None):
    n = w.shape[-1]
    n_main = n if n_main is None else n_main
    n_tail = n - n_main
    in_specs = [
        pl.BlockSpec((TM, D), lambda i, j: (i, 0)),
        _mod_spec(l, 3), _mod_spec(l, 4),
        pl.BlockSpec((1, 1, 1, D), lambda i, j: (l, 1, 0, 0)),
        pl.BlockSpec((1, D, tn), lambda i, j: (j_w, 0, j)),
    ]
    args = [x, mods, mods, norm_g4, w]
    out_specs = [pl.BlockSpec((TM, tn), lambda i, j: (i, j))]
    out_shape = [jax.ShapeDtypeStruct((NT, n_main), F32)]
    if n_tail:
        in_specs.append(pl.BlockSpec((1, D, n_tail), lambda i, j: (j_w, 0, n_main // n_tail)))
        args.append(w)
        out_specs.append(pl.BlockSpec((TM, n_tail), lambda i, j: (i, 0)))
        out_shape.append(jax.ShapeDtypeStruct((NT, n_tail), F32))
    return pl.pallas_call(
        functools.partial(_inproj_kernel, tail=bool(n_tail)),
        grid=(NT // TM, n_main // tn),
        in_specs=in_specs,
        out_specs=out_specs,
        out_shape=out_shape,
        scratch_shapes=[pltpu.VMEM((TM, D), BF16)],
        compiler_params=_cparams(("parallel", "arbitrary")),
        name="inproj",
    )(*args)


def _outproj_kernel(*refs, norm):
    if norm:
        y_ref, ng_ref, w_ref, x_ref, gt_ref, o_ref, yn_scr = refs
        j = pl.program_id(1)

        @pl.when(j == 0)
        def _():
            w = w_ref[0].astype(BF16)
            for r in range(TM // ROW_CHUNK):
                rows = pl.ds(r * ROW_CHUNK, ROW_CHUNK)
                y = y_ref[rows, :]
                yn = y * lax.rsqrt(jnp.mean(y * y, axis=-1, keepdims=True) + EPS) * ng_ref[...]
                yn = yn.astype(BF16)
                yn_scr[rows, :] = yn
                o_ref[rows, :] = x_ref[rows, :] + gt_ref[0, 0, 0] * _dot(yn, w)

        @pl.when(j > 0)
        def _():
            o_ref[...] = x_ref[...] + gt_ref[0, 0, 0] * _dot(yn_scr[...], w_ref[0].astype(BF16))
    else:
        y_ref, w_ref, x_ref, gt_ref, o_ref = refs
        o_ref[...] = x_ref[...] + gt_ref[0, 0, 0] * _dot(y_ref[...], w_ref[0].astype(BF16))


def _outproj(y, w, j_w, x, mods, l, norm_g=None):
    tn = 512
    k = y.shape[-1]
    norm = norm_g is not None
    y_spec = _resident_spec if norm else pl.BlockSpec
    in_specs = [y_spec((TM, k), lambda i, j: (i, 0))]
    args = [y]
    scratch = []
    if norm:
        in_specs.append(pl.BlockSpec((1, k), lambda i, j: (0, 0)))
        args.append(norm_g.reshape(1, k))
        scratch.append(pltpu.VMEM((TM, k), BF16))
    in_specs += [
        pl.BlockSpec((1, k, tn), lambda i, j: (j_w, 0, j)),
        pl.BlockSpec((TM, tn), lambda i, j: (i, j)),
        pl.BlockSpec((1, 1, 1, 1, tn), lambda i, j: (l, 5, _seg(i, TM), 0, j)),
    ]
    args += [w, x, mods]
    return pl.pallas_call(
        functools.partial(_outproj_kernel, norm=norm),
        grid=(NT // TM, D // tn),
        in_specs=in_specs,
        out_specs=pl.BlockSpec((TM, tn), lambda i, j: (i, j)),
        out_shape=jax.ShapeDtypeStruct((NT, D), F32),
        scratch_shapes=scratch,
        compiler_params=_cparams(("parallel", "arbitrary")),
        name="outproj",
    )(*args)


def _rmsnorm_kernel(x_ref, g_ref, o_ref):
    x = x_ref[...]
    o_ref[...] = x * lax.rsqrt(jnp.mean(x * x, axis=-1, keepdims=True) + EPS) * g_ref[...]


def _final_norm(x, g, row0, nrows):
    blk0 = row0 // TM
    return pl.pallas_call(
        _rmsnorm_kernel,
        grid=(nrows // TM,),
        in_specs=[pl.BlockSpec((TM, D), lambda i: (blk0 + i, 0)),
                  pl.BlockSpec((1, D), lambda i: (0, 0))],
        out_specs=pl.BlockSpec((TM, D), lambda i: (i, 0)),
        out_shape=jax.ShapeDtypeStruct((nrows, D), F32),
        compiler_params=_cparams(("parallel",)),
        name="final_norm",
    )(x, g.reshape(1, D))


def _split3(x):
    hi = x.astype(BF16)
    r = x - hi.astype(F32)
    mid = r.astype(BF16)
    lo = (r - mid.astype(F32)).astype(BF16)
    return hi, mid, lo


def _softplus(x):
    return jnp.maximum(x, 0.0) + jnp.log1p(jnp.exp(-jnp.abs(x)))


def _ssd_kernel(*refs, L, has_h0, emit_state):
    (z_ref, xs_ref, bm_ref, cm_ref, dt_ref, dtr_ref, cwx_ref, cwb_ref, cwc_ref,
     cbx_ref, cbb_ref, cbc_ref, dtb_ref, dtbr_ref, alr_ref, ala_ref, e_ref, dsk_ref) = refs[:18]
    pos = 18
    if has_h0:
        h0f_ref, h0b_ref = refs[pos:pos + 2]
        pos += 2
    y_ref = refs[pos]
    pos += 1
    if emit_state:
        hf_ref, hb_ref = refs[pos:pos + 2]
        pos += 2
    xs_s, bm_s, cm_s, dt_s, yb_s, ht_s = refs[pos:pos + 6]
    nc = L // CHUNK
    rowc = lax.broadcasted_iota(jnp.int32, (CHUNK, 1), 0)

    def prep(t, carry):
        r0 = pl.multiple_of(t * CHUNK, CHUNK)
        rows = pl.ds(r0, CHUNK)
        before = pl.ds(pl.multiple_of(jnp.maximum(r0 - 8, 0), 8), 8)
        after = pl.ds(pl.multiple_of(jnp.minimum(r0 + CHUNK, L - 8), 8), 8)
        for x_ref, w_ref, b_ref, out_s in ((xs_ref, cwx_ref, cbx_ref, xs_s),
                                           (bm_ref, cwb_ref, cbb_ref, bm_s),
                                           (cm_ref, cwc_ref, cbc_ref, cm_s)):
            x = x_ref[rows, :]
            prev = jnp.where(t == 0, 0.0, x_ref[before, :][7:8])
            nxt = jnp.where(t == nc - 1, 0.0, x_ref[after, :][0:1])
            xm = jnp.where(rowc == 0, prev, pltpu.roll(x, 1, 0))
            xp = jnp.where(rowc == CHUNK - 1, nxt, pltpu.roll(x, CHUNK - 1, 0))
            w = w_ref[...]
            out_s[rows, :] = _silu(w[0:1] * xm + w[1:2] * x + w[2:3] * xp + b_ref[...])
        dt_s[rows, :] = _softplus(dt_ref[rows, :] + dtb_ref[...])
        return carry

    lax.fori_loop(0, nc, prep, 0)
    a_all = -jnp.exp(ala_ref[...])
    a_r = -jnp.exp(alr_ref[0])
    dtb_r = dtbr_ref[0]

    if has_h0:
        ht_s[0] = jnp.transpose(h0f_ref[0])
        ht_s[1] = jnp.transpose(h0b_ref[0])
    else:
        ht_s[...] = jnp.zeros_like(ht_s)

    ii = lax.broadcasted_iota(jnp.int32, (CHUNK, CHUNK), 0)
    jj = lax.broadcasted_iota(jnp.int32, (CHUNK, CHUNK), 1)
    lower = ii >= jj
    upper = ii <= jj
    lane_lo = jj < SSM_P
    lower_k = jnp.concatenate([lower.astype(BF16)] * 3, axis=1)
    upper_k = jnp.concatenate([upper.astype(BF16)] * 3, axis=1)
    lower_r = jnp.concatenate([lower.astype(BF16)] * 3, axis=0)
    upper_r = jnp.concatenate([upper.astype(BF16)] * 3, axis=0)

    def direction(d, c):
        cc = c if d == 0 else nc - 1 - c
        r0 = pl.multiple_of(cc * CHUNK, CHUNK)
        xs_c = xs_s[pl.ds(r0, CHUNK), :]
        bm_b = bm_s[pl.ds(r0, CHUNK), :].astype(BF16)
        cm_b = cm_s[pl.ds(r0, CHUNK), :].astype(BF16)
        dt_c = dt_s[pl.ds(r0, CHUNK), :]
        dta_r = _softplus(dtr_ref[0, cc] + dtb_r) * a_r
        m_col, m_row, mask = (lower_k, upper_r, lower) if d == 0 else (upper_k, lower_r, upper)
        acum_c = _dot(m_col, jnp.concatenate(_split3(dt_c * a_all), axis=0))
        acum_r = _dot(jnp.concatenate(_split3(dta_r), axis=1), m_row)
        e_d = e_ref[0, d]
        dt_e = _dot(jnp.concatenate(_split3(dt_c)[:2], axis=1), e_d)
        acum_e = _dot(jnp.concatenate(_split3(acum_c)[:2], axis=1), e_d)
        last_e = acum_e[CHUNK - 1:CHUNK, :] if d == 0 else acum_e[0:1, :]
        cb = _dot_nt(cm_b, bm_b)
        xdt = xs_c * dt_e
        xdtd = (xdt * jnp.exp(last_e - acum_e)).astype(BF16)
        ht = ht_s[d]
        y_off = _dot(cm_b, ht.astype(BF16)) * jnp.exp(acum_e)
        ht_s[d] = ht * jnp.exp(last_e) + _dot_tn(bm_b, xdtd)
        for pair in range(4):
            ra = d * 8 + 2 * pair
            lanes = slice(pair * CHUNK, (pair + 1) * CHUNK)
            blk = acum_e[:, lanes]
            swp = pltpu.roll(blk, SSM_P, 1)
            ms = []
            for r, col in ((ra, jnp.where(lane_lo, blk, swp)), (ra + 1, jnp.where(lane_lo, swp, blk))):
                seg = col - acum_r[r:r + 1, :]
                decay = jnp.exp(jnp.where(mask, seg, -jnp.inf))
                ms.append((cb * decay).astype(BF16))
            m2 = jnp.concatenate(ms, axis=1)
            xb = xdt[:, lanes].astype(BF16)
            zero = jnp.zeros_like(xb)
            xbd = jnp.concatenate([jnp.where(lane_lo, xb, zero),
                                   jnp.where(lane_lo, zero, xb)], axis=0)
            out = y_ref if d == 0 else yb_s
            out[pl.ds(r0, CHUNK), lanes] = _dot(m2, xbd) + y_off[:, lanes]

    def chunk_body(c, carry):
        direction(0, c)
        direction(1, c)
        return carry

    lax.fori_loop(0, nc, chunk_body, 0, unroll=2)

    def finish(t, carry):
        rows = pl.ds(pl.multiple_of(t * CHUNK, CHUNK), CHUNK)
        y = y_ref[rows, :] + yb_s[rows, :] + dsk_ref[...] * xs_s[rows, :]
        y_ref[rows, :] = y * _silu(z_ref[rows, :])
        return carry

    lax.fori_loop(0, nc, finish, 0)

    if emit_state:
        hf_ref[0] = jnp.transpose(ht_s[0])
        hb_ref[0] = jnp.transpose(ht_s[1])


def _ssd_call(zx, dt_raw, dtr, conv_w, conv_b, dtb, dtb_r, al_r, al_all, e_sel, dsk, *,
              L, nb, row_blk0, h0=None, emit_state=False, y_prev=None):
    nc = L // CHUNK
    has_h0 = h0 is not None
    rb = lambda b: row_blk0 + b
    xoff = D_INNER // GW
    boff = 2 * D_INNER // SSM_N
    coff = boff + SSM_G
    in_specs = [
        pl.BlockSpec((L, GW), lambda b, g: (rb(b), g)),
        pl.BlockSpec((L, GW), lambda b, g: (rb(b), xoff + g)),
        pl.BlockSpec((L, SSM_N), lambda b, g: (rb(b), boff + g)),
        pl.BlockSpec((L, SSM_N), lambda b, g: (rb(b), coff + g)),
        pl.BlockSpec((L, 128), lambda b, g: (rb(b), 0)),
        pl.BlockSpec((1, nc, 16, CHUNK), lambda b, g: (g, rb(b), 0, 0)),
        pl.BlockSpec((3, GW), lambda b, g: (0, g)),
        pl.BlockSpec((3, SSM_N), lambda b, g: (0, D_INNER // SSM_N + g)),
        pl.BlockSpec((3, SSM_N), lambda b, g: (0, D_INNER // SSM_N + SSM_G + g)),
        pl.BlockSpec((1, GW), lambda b, g: (0, g)),
        pl.BlockSpec((1, SSM_N), lambda b, g: (0, D_INNER // SSM_N + g)),
        pl.BlockSpec((1, SSM_N), lambda b, g: (0, D_INNER // SSM_N + SSM_G + g)),
        pl.BlockSpec((1, 128), lambda b, g: (0, 0)),
        pl.BlockSpec((1, 16, 1), lambda b, g: (g, 0, 0)),
        pl.BlockSpec((1, 16, 1), lambda b, g: (g, 0, 0)),
        pl.BlockSpec((1, 128), lambda b, g: (0, 0)),
        pl.BlockSpec((1, 2, 256, GW), lambda b, g: (g, 0, 0, 0)),
        pl.BlockSpec((1, GW), lambda b, g: (0, g)),
    ]
    args = [zx, zx, zx, zx, dt_raw, dtr, conv_w, conv_w, conv_w, conv_b, conv_b, conv_b,
            dtb, dtb_r, al_r, al_all, e_sel, dsk]
    st_spec = pl.BlockSpec((1, GW, SSM_N), lambda b, g: (b, g, 0))
    if has_h0:
        in_specs += [st_spec, st_spec]
        args += [h0[0], h0[1]]
    out_specs = [pl.BlockSpec((L, GW), lambda b, g: (rb(b), g))]
    out_shape = [jax.ShapeDtypeStruct((NT, D_INNER), F32)]
    if emit_state:
        out_specs += [st_spec, st_spec]
        out_shape += [jax.ShapeDtypeStruct((nb, D_INNER, SSM_N), F32)] * 2
    body = functools.partial(_ssd_kernel, L=L, has_h0=has_h0, emit_state=emit_state)
    aliases = {}
    if y_prev is not None:
        in_specs.append(pl.BlockSpec(memory_space=pl.ANY))
        args.append(y_prev)
        aliases = {len(args) - 1: 0}
        body = _drop_ref(body, len(args) - 1)
    return pl.pallas_call(
        body,
        grid=(nb, SSM_G),
        in_specs=in_specs,
        out_specs=out_specs,
        out_shape=out_shape,
        scratch_shapes=[pltpu.VMEM((L, GW), F32), pltpu.VMEM((L, SSM_N), F32),
                        pltpu.VMEM((L, SSM_N), F32), pltpu.VMEM((L, 128), F32),
                        pltpu.VMEM((L, GW), F32),
                        pltpu.VMEM((2, SSM_N, GW), F32)],
        input_output_aliases=aliases,
        compiler_params=_cparams(("parallel", "parallel")),
        name="ssd",
    )(*args)


def _ssm_mixer(zx, dt_raw, h0_f, h0_b, conv_w, conv_b, dt_bias, a_log, d_skip):
    dt_g = dt_raw.reshape(NT, 2, SSM_G, 8).transpose(2, 0, 1, 3).reshape(SSM_G, NT, 16)
    dtr = dt_g.reshape(SSM_G, NT // CHUNK, CHUNK, 16).transpose(0, 1, 3, 2)

    def per_group(p):
        return p.reshape(2, SSM_G, 8).transpose(1, 0, 2).reshape(SSM_G, 16, 1)

    src = jnp.arange(2 * SSM_HEADS)[None, None, :, None]
    dst = (jnp.arange(2)[None, :, None, None] * SSM_HEADS + jnp.arange(SSM_G)[:, None, None, None] * 8
           + jnp.arange(GW)[None, None, None, :] // SSM_P)
    e_sel = (src == dst).astype(BF16)
    e_sel = jnp.concatenate([e_sel, e_sel], axis=2)
    dsk = jnp.repeat(d_skip, SSM_P).reshape(1, D_INNER)
    common = (zx, dt_raw, dtr, conv_w, conv_b.reshape(1, CONV_CH), dt_bias.reshape(1, 2 * SSM_HEADS),
              per_group(dt_bias), per_group(a_log), a_log.reshape(1, 2 * SSM_HEADS), e_sel, dsk)
    y, hf, hb = _ssd_call(*common, L=L_P, nb=NB_P, row_blk0=0, emit_state=True)
    h0 = (h0_f.reshape(NB_S, D_INNER, SSM_N), h0_b.reshape(NB_S, D_INNER, SSM_N))
    (y,) = _ssd_call(*common, L=L_S, nb=NB_S, row_blk0=NP // L_S, h0=h0, y_prev=y)
    st = (NB_P, SSM_HEADS, SSM_P, SSM_N)
    return y, hf.reshape(st), hb.reshape(st)


def _rope_tables():
    rows = L_S // GRID_W
    rowp = jnp.repeat(jnp.arange(rows, dtype=F32), GRID_W)
    colp = jnp.tile(jnp.arange(GRID_W, dtype=F32), rows)
    nf = DH // 4
    inv = ROPE_BASE ** (-jnp.arange(nf, dtype=F32) / nf)
    a0, a1 = rowp[:, None] * inv, colp[:, None] * inv
    c0, c1, s0, s1 = jnp.cos(a0), jnp.cos(a1), jnp.sin(a0), jnp.sin(a1)
    zz = jnp.zeros_like(s0)
    cos = jnp.concatenate([c0, c0, c1, c1], axis=-1)
    sin_next = jnp.concatenate([-s0, zz, -s1, zz], axis=-1)
    sin_prev = jnp.concatenate([zz, s0, zz, s1], axis=-1)
    return cos, sin_next, sin_prev


def _rope(x, cos, sin_next, sin_prev):
    return x * cos + pltpu.roll(x, 96, 1) * sin_next + pltpu.roll(x, 32, 1) * sin_prev


def _diff_kernel(*refs, L, latent, lambda_init, hb):
    q0_ref, q1_ref, k0_ref, k1_ref, v_ref = refs[:5]
    pos = 5
    if latent:
        ck0_ref, ck1_ref, cv_ref, cq_ref, snq_ref, spq_ref, ck_ref, snk_ref, spk_ref = refs[pos:pos + 9]
        pos += 9
    lam_ref, g_ref, o_ref = refs[pos:pos + 3]
    pos += 3
    if not latent:
        kout_ref, vout_ref = refs[pos:pos + 2]
        pos += 2
    kbuf, vbuf = refs[pos:pos + 2]
    off = PAST if latent else 0

    @pl.when(pl.program_id(2) == 0)
    def _():
        if not latent:
            kout_ref[:, 0:hb * DH] = k0_ref[...]
            kout_ref[:, hb * DH:2 * hb * DH] = k1_ref[...]
            vout_ref[...] = v_ref[...]
        for hh in range(hb):
            c1 = slice(hh * DH, (hh + 1) * DH)
            c2 = slice(hh * 2 * DH, (hh + 1) * 2 * DH)
            if latent:
                kbuf[0, hh, 0:PAST, :] = ck0_ref[0, :, c1].astype(BF16)
                kbuf[1, hh, 0:PAST, :] = ck1_ref[0, :, c1].astype(BF16)
                vbuf[hh, 0:PAST, :] = cv_ref[0, :, c2].astype(BF16)
            for m, k_ref in enumerate((k0_ref, k1_ref)):
                k = k_ref[:, c1]
                if latent:
                    k = _rope(k, ck_ref[...], snk_ref[...], spk_ref[...])
                kbuf[m, hh, off:off + L, :] = k.astype(BF16)
            vbuf[hh, off:off + L, :] = v_ref[:, c2].astype(BF16)

    lv = lam_ref[...]
    lam = (jnp.exp(jnp.sum(lv[0:1] * lv[1:2], axis=1, keepdims=True))
           - jnp.exp(jnp.sum(lv[2:3] * lv[3:4], axis=1, keepdims=True)) + lambda_init)
    for hh in range(hb):
        c1 = slice(hh * DH, (hh + 1) * DH)
        c2 = slice(hh * 2 * DH, (hh + 1) * 2 * DH)
        ps = []
        for m, q_ref in enumerate((q0_ref, q1_ref)):
            q = q_ref[:, c1]
            if latent:
                q = _rope(q, cq_ref[...], snq_ref[...], spq_ref[...])
            s = _dot_nt(q.astype(BF16), kbuf[m, hh]) * (DH ** -0.5)
            e = jnp.exp(s - jnp.max(s, axis=-1, keepdims=True))
            ps.append(e / jnp.sum(e, axis=-1, keepdims=True))
        p = (ps[0] - lam * ps[1]).astype(BF16)
        o = _dot(p, vbuf[hh])
        o = o * lax.rsqrt(jnp.mean(o * o, axis=-1, keepdims=True) + EPS) * g_ref[...]
        o_ref[:, c2] = (o * (1.0 - lambda_init)).astype(BF16)


def _diff_call(qkv, lam_p, subln_g, lambda_init, *, L, nb, row_blk0, qb, hb, cache=None, tables=None,
               o_prev=None):
    latent = cache is not None
    nq = L // qb
    nh = DIFF_H // hb
    kt = L + (PAST if latent else 0)
    rq = lambda b, qi: (row_blk0 + b) * nq + qi
    rs = lambda b: row_blk0 + b
    in_specs = [
        pl.BlockSpec((qb, hb * DH), lambda b, h, qi: (rq(b, qi), h)),
        pl.BlockSpec((qb, hb * DH), lambda b, h, qi: (rq(b, qi), nh + h)),
        pl.BlockSpec((L, hb * DH), lambda b, h, qi: (rs(b), 2 * nh + h)),
        pl.BlockSpec((L, hb * DH), lambda b, h, qi: (rs(b), 3 * nh + h)),
        pl.BlockSpec((L, hb * 2 * DH), lambda b, h, qi: (rs(b), 2 * nh + h)),
    ]
    args = [qkv] * 5
    if latent:
        ck, cv = cache
        in_specs += [
            pl.BlockSpec((1, PAST, hb * DH), lambda b, h, qi: (b, 0, h)),
            pl.BlockSpec((1, PAST, hb * DH), lambda b, h, qi: (b, 0, nh + h)),
            pl.BlockSpec((1, PAST, hb * 2 * DH), lambda b, h, qi: (b, 0, h)),
        ]
        args += [ck, ck, cv]
        in_specs += [pl.BlockSpec((qb, DH), lambda b, h, qi: (qi, 0))] * 3
        in_specs += [pl.BlockSpec((L, DH), lambda b, h, qi: (0, 0))] * 3
        args += list(tables) * 2
    in_specs += [pl.BlockSpec((4, DH), lambda b, h, qi: (0, 0)),
                 pl.BlockSpec((1, 2 * DH), lambda b, h, qi: (0, 0))]
    args += [lam_p, subln_g.reshape(1, 2 * DH)]
    body = functools.partial(_diff_kernel, L=L, latent=latent, lambda_init=lambda_init, hb=hb)
    aliases = {}
    if o_prev is not None:
        in_specs.append(pl.BlockSpec(memory_space=pl.ANY))
        args.append(o_prev)
        aliases = {len(args) - 1: 0}
        body = _drop_ref(body, len(args) - 1)
    out_specs = [pl.BlockSpec((qb, hb * 2 * DH), lambda b, h, qi: (rq(b, qi), h))]
    out_shape = [jax.ShapeDtypeStruct((NT, D), BF16)]
    if not latent:
        assert hb == DIFF_H and nq == 1 and row_blk0 == 0
        out_specs += [pl.BlockSpec((L, D), lambda b, h, qi: (b, 0))] * 2
        out_shape += [jax.ShapeDtypeStruct((nb * L, D), F32)] * 2
    return pl.pallas_call(
        body,
        grid=(nb, nh, nq),
        in_specs=in_specs,
        out_specs=out_specs,
        out_shape=out_shape,
        scratch_shapes=[pltpu.VMEM((2, hb, kt, DH), BF16), pltpu.VMEM((hb, kt, 2 * DH), BF16)],
        input_output_aliases=aliases,
        compiler_params=_cparams(("parallel", "parallel", "arbitrary")),
        name="diff_attn",
    )(*args)


def _win_kernel(*refs, L, latent, gb):
    q_ref, k_ref, v_ref = refs[:3]
    pos = 3
    if latent:
        ck_ref, cv_ref, cq_ref, snq_ref, spq_ref, ckk_ref, snk_ref, spk_ref = refs[pos:pos + 8]
        pos += 8
    sink_ref, o_ref = refs[pos:pos + 2]
    pos += 2
    if not latent:
        kout_ref, vout_ref = refs[pos:pos + 2]
        pos += 2
        kout_ref[...] = k_ref[...]
        vout_ref[...] = v_ref[...]
    g = pl.program_id(1)
    n = pl.program_id(2)
    scale = DH ** -0.5
    qpk = WIN_H // WIN_KV

    if latent:
        kbuf, vbuf = refs[pos:pos + 2]
        koff = PAST + WINDOW

        @pl.when(n == 0)
        def _():
            zpad = jnp.zeros((WINDOW, DH), BF16)
            for gg in range(gb):
                cols = slice(gg * DH, (gg + 1) * DH)
                for buf, c_ref, x_ref, rope in ((kbuf, ck_ref, k_ref, True), (vbuf, cv_ref, v_ref, False)):
                    buf[gg, 0:PAST, :] = c_ref[0, :, cols].astype(BF16)
                    buf[gg, PAST:koff, :] = zpad
                    x = x_ref[:, cols]
                    if rope:
                        x = _rope(x, ckk_ref[...], snk_ref[...], spk_ref[...])
                    buf[gg, koff:koff + L, :] = x.astype(BF16)
                    buf[gg, koff + L:koff + L + WINDOW, :] = zpad

        band0 = pl.multiple_of(PAST + n * WINDOW, WINDOW)
        qi_ = lax.broadcasted_iota(jnp.int32, (WINDOW, 3 * WINDOW), 0)
        kj_ = lax.broadcasted_iota(jnp.int32, (WINDOW, 3 * WINDOW), 1)
        kpos = (n - 1) * WINDOW + kj_
        delta = kj_ - qi_
        valid = (delta >= 0) & (delta <= 2 * WINDOW) & (kpos >= 0) & (kpos < L)

    for gg in range(gb):
        if latent:
            kc, vc = kbuf[gg, 0:PAST, :], vbuf[gg, 0:PAST, :]
            kb, vb = kbuf[gg, pl.ds(band0, 3 * WINDOW), :], vbuf[gg, pl.ds(band0, 3 * WINDOW), :]
        else:
            kc = k_ref[:, gg * DH:(gg + 1) * DH].astype(BF16)
            vc = v_ref[:, gg * DH:(gg + 1) * DH].astype(BF16)
        for r in range(qpk):
            cols = slice((gg * qpk + r) * DH, (gg * qpk + r + 1) * DH)
            q = q_ref[:, cols]
            if latent:
                q = _rope(q, cq_ref[...], snq_ref[...], spq_ref[...])
            qb_ = q.astype(BF16)
            sink = sink_ref[(g * gb + gg) * qpk + r]
            s_c = _dot_nt(qb_, kc) * scale
            mx = jnp.maximum(jnp.max(s_c, axis=-1, keepdims=True), sink)
            if latent:
                s_b = jnp.where(valid, _dot_nt(qb_, kb) * scale, -jnp.inf)
                mx = jnp.maximum(mx, jnp.max(s_b, axis=-1, keepdims=True))
            e_c = jnp.exp(s_c - mx)
            den = jnp.sum(e_c, axis=-1, keepdims=True) + jnp.exp(sink - mx)
            if latent:
                e_b = jnp.exp(s_b - mx)
                den = den + jnp.sum(e_b, axis=-1, keepdims=True)
            o = _dot((e_c / den).astype(BF16), vc)
            if latent:
                o = o + _dot((e_b / den).astype(BF16), vb)
            o_ref[:, cols] = o.astype(BF16)


def _win_call(qkv, sink, *, L, nb, row_blk0, qb, gb, cache=None, tables=None, o_prev=None):
    latent = cache is not None
    nq = L // qb
    ng = WIN_KV // gb
    qw = gb * (WIN_H // WIN_KV) * DH
    koff = WIN_H // gb
    voff = (WIN_H + WIN_KV) // gb
    rq = lambda b, qi: (row_blk0 + b) * nq + qi
    rs = lambda b: row_blk0 + b
    in_specs = [
        pl.BlockSpec((qb, qw), lambda b, g, qi: (rq(b, qi), g)),
        pl.BlockSpec((L, gb * DH), lambda b, g, qi: (rs(b), koff + g)),
        pl.BlockSpec((L, gb * DH), lambda b, g, qi: (rs(b), voff + g)),
    ]
    args = [qkv] * 3
    scratch = []
    if latent:
        ck, cv = cache
        in_specs += [pl.BlockSpec((1, PAST, gb * DH), lambda b, g, qi: (b, 0, g))] * 2
        args += [ck, cv]
        in_specs += [pl.BlockSpec((qb, DH), lambda b, g, qi: (qi, 0))] * 3
        in_specs += [pl.BlockSpec((L, DH), lambda b, g, qi: (0, 0))] * 3
        args += list(tables) * 2
        kt = PAST + L + 2 * WINDOW
        scratch = [pltpu.VMEM((gb, kt, DH), BF16), pltpu.VMEM((gb, kt, DH), BF16)]
    in_specs.append(pl.BlockSpec(memory_space=pltpu.SMEM))
    args.append(sink)
    body = functools.partial(_win_kernel, L=L, latent=latent, gb=gb)
    aliases = {}
    if o_prev is not None:
        in_specs.append(pl.BlockSpec(memory_space=pl.ANY))
        args.append(o_prev)
        aliases = {len(args) - 1: 0}
        body = _drop_ref(body, len(args) - 1)
    out_specs = [pl.BlockSpec((qb, qw), lambda b, g, qi: (rq(b, qi), g))]
    out_shape = [jax.ShapeDtypeStruct((NT, D), BF16)]
    if not latent:
        assert gb == WIN_KV and nq == 1 and row_blk0 == 0
        out_specs += [pl.BlockSpec((L, gb * DH), lambda b, g, qi: (b, 0))] * 2
        out_shape += [jax.ShapeDtypeStruct((nb * L, gb * DH), F32)] * 2
    return pl.pallas_call(
        body,
        grid=(nb, ng, nq),
        in_specs=in_specs,
        out_specs=out_specs,
        out_shape=out_shape,
        scratch_shapes=scratch,
        input_output_aliases=aliases,
        compiler_params=_cparams(("parallel", "parallel", "arbitrary")),
        name="win_attn",
    )(*args)


def kernel(x_prompt, x_sample, state_l0_fwd, state_l0_bwd, cache_l1_k, cache_l1_v, cache_l2_k, cache_l2_v, state_l3_fwd, state_l3_bwd, c, c_ctx, norm_g, w_ada, b_ada, ffn1_w_in, ffn1_w_out, ffn2_w_in, ffn2_w_out, ssm_w_in, ssm_conv_w, ssm_conv_b, ssm_dt_bias, ssm_a_log, ssm_d, ssm_norm_g, ssm_w_out, diff_w_qkv, diff_lambda, diff_subln_g, diff_w_out, win_w_qkv, win_sink, win_w_out, final_norm_g):
    x = _stack_tokens(x_prompt.reshape(NP, D), x_sample.reshape(NS, D))
    cv8 = jnp.concatenate([c_ctx[None, :], c, jnp.zeros((8 - 1 - NB_S, D), F32)], axis=0)
    mods = _modulation(cv8, w_ada, b_ada)
    norm_g4 = norm_g.reshape(DEPTH, 3, 1, D)
    tables = _rope_tables()
    ssm_states = [(state_l0_fwd, state_l0_bwd), (state_l3_fwd, state_l3_bwd)]
    new_state = []
    for l in range(DEPTH):
        m, j = l % 3, l // 3
        x = _ffn(x, mods, norm_g4, ffn1_w_in, ffn1_w_out, l, 0)
        if m == 0:
            zx, dt_raw = _inproj(x, mods, norm_g4, ssm_w_in, l, j, 1024, n_main=D_INNER + CONV_CH)
            y, hf, hb = _ssm_mixer(zx, dt_raw, *ssm_states[j], ssm_conv_w[j], ssm_conv_b[j],
                                   ssm_dt_bias[j], ssm_a_log[j], ssm_d[j])
            x = _outproj(y, ssm_w_out, j, x, mods, l, norm_g=ssm_norm_g[j])
            new_state += [hf, hb]
        elif m == 1:
            lambda_init = 0.8 - 0.6 * math.exp(-0.3 * l)
            (qkv,) = _inproj(x, mods, norm_g4, diff_w_qkv, l, j, 512)
            o, k_new, v_new = _diff_call(qkv, diff_lambda[j], diff_subln_g[j], lambda_init,
                                         L=L_P, nb=NB_P, row_blk0=0, qb=L_P, hb=DIFF_H)
            cache = (cache_l1_k.reshape(NB_S, PAST, D), cache_l1_v.reshape(NB_S, PAST, D))
            (o,) = _diff_call(qkv, diff_lambda[j], diff_subln_g[j], lambda_init,
                              L=L_S, nb=NB_S, row_blk0=NP // L_S, qb=256, hb=4, cache=cache,
                              tables=tables, o_prev=o)
            x = _outproj(o, diff_w_out, j, x, mods, l)
            new_state += [k_new.reshape(NB_P, L_P, 2, DIFF_H, DH),
                          v_new.reshape(NB_P, L_P, DIFF_H, 2 * DH)]
        else:
            (qkv,) = _inproj(x, mods, norm_g4, win_w_qkv, l, j, 512)
            o, k_new, v_new = _win_call(qkv, win_sink[j], L=L_P, nb=NB_P, row_blk0=0, qb=L_P,
                                        gb=WIN_KV)
            kvw = WIN_KV * DH
            cache = (cache_l2_k.reshape(NB_S, PAST, kvw), cache_l2_v.reshape(NB_S, PAST, kvw))
            (o,) = _win_call(qkv, win_sink[j], L=L_S, nb=NB_S, row_blk0=NP // L_S, qb=WINDOW, gb=4,
                             cache=cache, tables=tables, o_prev=o)
            x = _outproj(o, win_w_out, j, x, mods, l)
            new_state += [k_new.reshape(NB_P, L_P, WIN_KV, DH), v_new.reshape(NB_P, L_P, WIN_KV, DH)]
        x = _ffn(x, mods, norm_g4, ffn2_w_in, ffn2_w_out, l, 2)
    y_p = _final_norm(x, final_norm_g, 0, NP)
    y_s = _final_norm(x, final_norm_g, NP, NS)
    s0f, s0b, k1, v1, k2, v2, s3f, s3b = new_state
    return (y_p.reshape(NB_P, L_P, D), y_s.reshape(NB_S, L_S, D),
            s0f, s0b, k1, v1, k2, v2, s3f, s3b)
```

```python
import functools
import math

import jax
import jax.numpy as jnp
from jax import lax
from jax.experimental import pallas as pl
from jax.experimental.pallas import tpu as pltpu

F32 = jnp.float32
BF16 = jnp.bfloat16

D = 2048
NB_P, L_P = 16, 256
NB_S, L_S = 2, 1024
PAST = 256
NP = NB_P * L_P
NS = NB_S * L_S
NT = NP + NS
DEPTH = 4
N_MOD = 9
D_FF = 5632
EPS = 1e-6
GRID_W = 64
ROPE_BASE = 10000.0

D_INNER = 4096
SSM_HEADS = 64
SSM_P = 64
SSM_G = 8
SSM_N = 128
CONV_CH = D_INNER + 2 * SSM_G * SSM_N
SSM_IN = D_INNER + CONV_CH + 2 * SSM_HEADS
CHUNK = 128
GW = D_INNER // SSM_G

DIFF_H = 8
DH = 128
WIN_H = 16
WIN_KV = 4
WINDOW = 128

TM = 1024
ROW_CHUNK = 256
STACK_ROWS = 512
VMEM_LIMIT = 60 * 1024 * 1024


def _resident_spec(block_shape, index_map):
    return pl.BlockSpec(block_shape, index_map, pipeline_mode=pl.Buffered(1))


def _cparams(sem):
    return pltpu.CompilerParams(dimension_semantics=sem, vmem_limit_bytes=VMEM_LIMIT)


def _drop_ref(kernel_fn, idx):
    def body(*refs):
        return kernel_fn(*(refs[:idx] + refs[idx + 1:]))
    return body


def _silu(x):
    return x * (0.5 * jnp.tanh(0.5 * x) + 0.5)


def _dot(a, b):
    return jnp.dot(a, b, preferred_element_type=F32)


def _dot_nt(a, b):
    return lax.dot_general(a, b, (((1,), (1,)), ((), ())), preferred_element_type=F32)


def _dot_tn(a, b):
    return lax.dot_general(a, b, (((0,), (0,)), ((), ())), preferred_element_type=F32)


def _seg(i, tm):
    start = i * tm
    return jnp.where(start < NP, 0, 1 + (start - NP) // L_S)


def _modnorm(x, g, shift, scale):
    y = x * lax.rsqrt(jnp.mean(x * x, axis=-1, keepdims=True) + EPS) * g
    return y * (1.0 + scale) + shift


def _stack_kernel(xp_ref, xs_ref, o_ref):
    i = pl.program_id(0)

    @pl.when(i < NP // STACK_ROWS)
    def _():
        o_ref[...] = xp_ref[...]

    @pl.when(i >= NP // STACK_ROWS)
    def _():
        o_ref[...] = xs_ref[...]


def _stack_tokens(xp, xs):
    npb = NP // STACK_ROWS
    return pl.pallas_call(
        _stack_kernel,
        grid=(NT // STACK_ROWS,),
        in_specs=[pl.BlockSpec((STACK_ROWS, D), lambda i: (jnp.minimum(i, npb - 1), 0)),
                  pl.BlockSpec((STACK_ROWS, D), lambda i: (jnp.maximum(i - npb, 0), 0))],
        out_specs=pl.BlockSpec((STACK_ROWS, D), lambda i: (i, 0)),
        out_shape=jax.ShapeDtypeStruct((NT, D), F32),
        compiler_params=_cparams(("arbitrary",)),
        name="stack_tokens",
    )(xp, xs)


def _mod_kernel(cv_ref, w_ref, b_ref, o_ref):
    @pl.when(pl.program_id(2) == 0)
    def _():
        o_ref[0, 0] = jnp.broadcast_to(b_ref[0, 0], (8, D))

    cv = cv_ref[...]
    o_ref[0, 0] += _dot(_silu(cv).astype(BF16), w_ref[0].astype(BF16))


def _modulation(cv8, w_ada, b_ada):
    kc = 1024
    out = pl.pallas_call(
        _mod_kernel,
        grid=(DEPTH, N_MOD, D // kc),
        in_specs=[
            pl.BlockSpec((8, kc), lambda l, j, k: (0, k)),
            pl.BlockSpec((1, kc, D), lambda l, j, k: (l, k, j)),
            pl.BlockSpec((1, 1, 1, D), lambda l, j, k: (l, j, 0, 0)),
        ],
        out_specs=pl.BlockSpec((1, 1, 8, D), lambda l, j, k: (l, j, 0, 0)),
        out_shape=jax.ShapeDtypeStruct((DEPTH, N_MOD, 8, D), F32),
        compiler_params=_cparams(("parallel", "parallel", "arbitrary")),
        name="modulation",
    )(cv8, w_ada, b_ada.reshape(DEPTH, N_MOD, 1, D))
    return out.reshape(DEPTH, N_MOD, 8, 1, D)


def _mod_spec(l, k):
    return pl.BlockSpec((1, 1, 1, 1, D), lambda i, j: (l, k, _seg(i, TM), 0, 0))


def _ffn_kernel(x_ref, sh_ref, sc_ref, gt_ref, g_ref, wg_ref, wu_ref, wo_ref, o_ref, h_scr):
    j = pl.program_id(1)

    def weights():
        return wg_ref[0].astype(BF16), wu_ref[0].astype(BF16), wo_ref[0].astype(BF16)

    def hidden_chunk(h, wg, wu, wo):
        act = (_silu(_dot(h, wg)) * _dot(h, wu)).astype(BF16)
        return _dot(act, wo)

    @pl.when(j == 0)
    def _():
        w = weights()
        for r in range(TM // ROW_CHUNK):
            rows = pl.ds(r * ROW_CHUNK, ROW_CHUNK)
            h = _modnorm(x_ref[rows, :], g_ref[0, 0], sh_ref[0, 0, 0], sc_ref[0, 0, 0]).astype(BF16)
            h_scr[rows, :] = h
            o_ref[rows, :] = hidden_chunk(h, *w)

    last = pl.num_programs(1) - 1

    @pl.when(jnp.logical_and(j > 0, j < last))
    def _():
        o_ref[...] += hidden_chunk(h_scr[...], *weights())

    @pl.when(j == last)
    def _():
        ffn = o_ref[...] + hidden_chunk(h_scr[...], *weights())
        o_ref[...] = x_ref[...] + 0.5 * gt_ref[0, 0, 0] * ffn


def _ffn(x, mods, norm_g4, w_in, w_out, l, sub):
    tf = 256
    nf = D_FF // tf
    return pl.pallas_call(
        _ffn_kernel,
        grid=(NT // TM, nf),
        in_specs=[
            pl.BlockSpec((TM, D), lambda i, j: (i, 0)),
            _mod_spec(l, 3 * sub), _mod_spec(l, 3 * sub + 1), _mod_spec(l, 3 * sub + 2),
            pl.BlockSpec((1, 1, 1, D), lambda i, j: (l, sub, 0, 0)),
            pl.BlockSpec((1, D, tf), lambda i, j: (l, 0, j)),
            pl.BlockSpec((1, D, tf), lambda i, j: (l, 0, j + nf)),
            pl.BlockSpec((1, tf, D), lambda i, j: (l, j, 0)),
        ],
        out_specs=pl.BlockSpec((TM, D), lambda i, j: (i, 0)),
        out_shape=jax.ShapeDtypeStruct((NT, D), F32),
        scratch_shapes=[pltpu.VMEM((TM, D), BF16)],
        compiler_params=_cparams(("parallel", "arbitrary")),
        name="ffn",
    )(x, mods, mods, mods, norm_g4, w_in, w_in, w_out)


def _inproj_kernel(*refs, tail):
    if tail:
        x_ref, sh_ref, sc_ref, g_ref, w_ref, wt_ref, o_ref, ot_ref, h_scr = refs
    else:
        x_ref, sh_ref, sc_ref, g_ref, w_ref, o_ref, h_scr = refs
    j = pl.program_id(1)

    @pl.when(j == 0)
    def _():
        w = w_ref[0].astype(BF16)
        if tail:
            wt = wt_ref[0].astype(BF16)
        for r in range(TM // ROW_CHUNK):
            rows = pl.ds(r * ROW_CHUNK, ROW_CHUNK)
            h = _modnorm(x_ref[rows, :], g_ref[0, 0], sh_ref[0, 0, 0], sc_ref[0, 0, 0]).astype(BF16)
            h_scr[rows, :] = h
            o_ref[rows, :] = _dot(h, w)
            if tail:
                ot_ref[rows, :] = _dot(h, wt)

    @pl.when(j > 0)
    def _():
        o_ref[...] = _dot(h_scr[...], w_ref[0].astype(BF16))


def _inproj(x, mods, norm_g4, w, l, j_w, tn, n_main=None):
    n = w.shape[-1]
    n_main = n if n_main is None else n_main
    n_tail = n - n_main
    in_specs = [
        pl.BlockSpec((TM, D), lambda i, j: (i, 0)),
        _mod_spec(l, 3), _mod_spec(l, 4),
        pl.BlockSpec((1, 1, 1, D), lambda i, j: (l, 1, 0, 0)),
        pl.BlockSpec((1, D, tn), lambda i, j: (j_w, 0, j)),
    ]
    args = [x, mods, mods, norm_g4, w]
    out_specs = [pl.BlockSpec((TM, tn), lambda i, j: (i, j))]
    out_shape = [jax.ShapeDtypeStruct((NT, n_main), F32)]
    if n_tail:
        in_specs.append(pl.BlockSpec((1, D, n_tail), lambda i, j: (j_w, 0, n_main // n_tail)))
        args.append(w)
        out_specs.append(pl.BlockSpec((TM, n_tail), lambda i, j: (i, 0)))
        out_shape.append(jax.ShapeDtypeStruct((NT, n_tail), F32))
    return pl.pallas_call(
        functools.partial(_inproj_kernel, tail=bool(n_tail)),
        grid=(NT // TM, n_main // tn),
        in_specs=in_specs,
        out_specs=out_specs,
        out_shape=out_shape,
        scratch_shapes=[pltpu.VMEM((TM, D), BF16)],
        compiler_params=_cparams(("parallel", "arbitrary")),
        name="inproj",
    )(*args)


def _outproj_kernel(*refs, norm):
    if norm:
        y_ref, ng_ref, w_ref, x_ref, gt_ref, o_ref, yn_scr = refs
        j = pl.program_id(1)

        @pl.when(j == 0)
        def _():
            w = w_ref[0].astype(BF16)
            for r in range(TM // ROW_CHUNK):
                rows = pl.ds(r * ROW_CHUNK, ROW_CHUNK)
                y = y_ref[rows, :].astype(F32)
                yn = y * lax.rsqrt(jnp.mean(y * y, axis=-1, keepdims=True) + EPS) * ng_ref[...]
                yn = yn.astype(BF16)
                yn_scr[rows, :] = yn
                o_ref[rows, :] = x_ref[rows, :] + gt_ref[0, 0, 0] * _dot(yn, w)

        @pl.when(j > 0)
        def _():
            o_ref[...] = x_ref[...] + gt_ref[0, 0, 0] * _dot(yn_scr[...], w_ref[0].astype(BF16))
    else:
        y_ref, w_ref, x_ref, gt_ref, o_ref = refs
        o_ref[...] = x_ref[...] + gt_ref[0, 0, 0] * _dot(y_ref[...], w_ref[0].astype(BF16))


def _outproj(y, w, j_w, x, mods, l, norm_g=None):
    tn = 512
    k = y.shape[-1]
    norm = norm_g is not None
    in_specs = [pl.BlockSpec((TM, k), lambda i, j: (i, 0))]
    args = [y]
    scratch = []
    if norm:
        in_specs.append(pl.BlockSpec((1, k), lambda i, j: (0, 0)))
        args.append(norm_g.reshape(1, k))
        scratch.append(pltpu.VMEM((TM, k), BF16))
    in_specs += [
        pl.BlockSpec((1, k, tn), lambda i, j: (j_w, 0, j)),
        pl.BlockSpec((TM, tn), lambda i, j: (i, j)),
        pl.BlockSpec((1, 1, 1, 1, tn), lambda i, j: (l, 5, _seg(i, TM), 0, j)),
    ]
    args += [w, x, mods]
    return pl.pallas_call(
        functools.partial(_outproj_kernel, norm=norm),
        grid=(NT // TM, D // tn),
        in_specs=in_specs,
        out_specs=pl.BlockSpec((TM, tn), lambda i, j: (i, j)),
        out_shape=jax.ShapeDtypeStruct((NT, D), F32),
        scratch_shapes=scratch,
        compiler_params=_cparams(("parallel", "arbitrary")),
        name="outproj",
    )(*args)


def _rmsnorm_kernel(x_ref, g_ref, o_ref):
    x = x_ref[...]
    o_ref[...] = x * lax.rsqrt(jnp.mean(x * x, axis=-1, keepdims=True) + EPS) * g_ref[...]


def _final_norm(x, g, row0, nrows):
    blk0 = row0 // TM
    return pl.pallas_call(
        _rmsnorm_kernel,
        grid=(nrows // TM,),
        in_specs=[pl.BlockSpec((TM, D), lambda i: (blk0 + i, 0)),
                  pl.BlockSpec((1, D), lambda i: (0, 0))],
        out_specs=pl.BlockSpec((TM, D), lambda i: (i, 0)),
        out_shape=jax.ShapeDtypeStruct((nrows, D), F32),
        compiler_params=_cparams(("parallel",)),
        name="final_norm",
    )(x, g.reshape(1, D))


def _split3(x):
    hi = x.astype(BF16)
    r = x - hi.astype(F32)
    mid = r.astype(BF16)
    lo = (r - mid.astype(F32)).astype(BF16)
    return hi, mid, lo


def _softplus(x):
    return jnp.maximum(x, 0.0) + jnp.log1p(jnp.exp(-jnp.abs(x)))


def _ssd_kernel(*refs, L, has_h0, emit_state):
    (z_ref, xs_ref, bm_ref, cm_ref, dt_ref, dtr_ref, cwx_ref, cwb_ref, cwc_ref,
     cbx_ref, cbb_ref, cbc_ref, dtb_ref, dtbr_ref, alr_ref, ala_ref, e_ref, dsk_ref) = refs[:18]
    pos = 18
    if has_h0:
        h0f_ref, h0b_ref = refs[pos:pos + 2]
        pos += 2
    y_ref = refs[pos]
    pos += 1
    if emit_state:
        hf_ref, hb_ref = refs[pos:pos + 2]
        pos += 2
    xs_s, bm_s, cm_s, dt_s, yf_s, yb_s, ht_s, dte_s, ace_s, acr_s = refs[pos:pos + 10]
    nc = L // CHUNK
    rowc = lax.broadcasted_iota(jnp.int32, (CHUNK, 1), 0)
    a_all = -jnp.exp(ala_ref[...])
    a_r = -jnp.exp(alr_ref[0])
    dtb_r = dtbr_ref[0]
    ii = lax.broadcasted_iota(jnp.int32, (CHUNK, CHUNK), 0)
    jj = lax.broadcasted_iota(jnp.int32, (CHUNK, CHUNK), 1)
    lower = ii >= jj
    upper = ii <= jj
    lane_lo = jj < SSM_P
    lower_k = jnp.concatenate([lower.astype(BF16)] * 3, axis=1)
    upper_k = jnp.concatenate([upper.astype(BF16)] * 3, axis=1)
    lower_r = jnp.concatenate([lower.astype(BF16)] * 3, axis=0)
    upper_r = jnp.concatenate([upper.astype(BF16)] * 3, axis=0)

    def prep(t, carry):
        r0 = pl.multiple_of(t * CHUNK, CHUNK)
        rows = pl.ds(r0, CHUNK)
        before = pl.ds(pl.multiple_of(jnp.maximum(r0 - 8, 0), 8), 8)
        after = pl.ds(pl.multiple_of(jnp.minimum(r0 + CHUNK, L - 8), 8), 8)
        for x_ref, w_ref, b_ref, out_s in ((xs_ref, cwx_ref, cbx_ref, xs_s),
                                           (bm_ref, cwb_ref, cbb_ref, bm_s),
                                           (cm_ref, cwc_ref, cbc_ref, cm_s)):
            x = x_ref[rows, :]
            prev = jnp.where(t == 0, 0.0, x_ref[before, :][7:8])
            nxt = jnp.where(t == nc - 1, 0.0, x_ref[after, :][0:1])
            xm = jnp.where(rowc == 0, prev, pltpu.roll(x, 1, 0))
            xp = jnp.where(rowc == CHUNK - 1, nxt, pltpu.roll(x, CHUNK - 1, 0))
            w = w_ref[...]
            out_s[rows, :] = _silu(w[0:1] * xm + w[1:2] * x + w[2:3] * xp + b_ref[...])
        dt_s[rows, :] = _softplus(dt_ref[rows, :] + dtb_ref[...])
        return carry

    lax.fori_loop(0, nc, prep, 0)

    def decays(t, carry):
        rows = pl.ds(pl.multiple_of(t * CHUNK, CHUNK), CHUNK)
        dt_c = dt_s[rows, :]
        dta_c = jnp.concatenate(_split3(dt_c * a_all), axis=0)
        dt_hm = jnp.concatenate(_split3(dt_c)[:2], axis=1)
        dta_r = _softplus(dtr_ref[0, t] + dtb_r) * a_r
        dta_r = jnp.concatenate(_split3(dta_r), axis=1)
        for d in range(2):
            m_col, m_row = (lower_k, upper_r) if d == 0 else (upper_k, lower_r)
            acum_c = _dot(m_col, dta_c)
            acr_s[d, t] = _dot(dta_r, m_row)
            e_d = e_ref[0, d]
            dte_s[d, rows, :] = _dot(dt_hm, e_d)
            ace_s[d, rows, :] = _dot(jnp.concatenate(_split3(acum_c)[:2], axis=1), e_d)
        return carry

    lax.fori_loop(0, nc, decays, 0, unroll=2)

    if has_h0:
        ht_s[0] = jnp.transpose(h0f_ref[0])
        ht_s[1] = jnp.transpose(h0b_ref[0])
    else:
        ht_s[...] = jnp.zeros_like(ht_s)

    def direction(d, c):
        cc = c if d == 0 else nc - 1 - c
        r0 = pl.multiple_of(cc * CHUNK, CHUNK)
        xs_c = xs_s[pl.ds(r0, CHUNK), :]
        bm_b = bm_s[pl.ds(r0, CHUNK), :].astype(BF16)
        cm_b = cm_s[pl.ds(r0, CHUNK), :].astype(BF16)
        mask = lower if d == 0 else upper
        dt_e = dte_s[d, pl.ds(r0, CHUNK), :]
        acum_e = ace_s[d, pl.ds(r0, CHUNK), :]
        acum_r = acr_s[d, cc]
        last_e = acum_e[CHUNK - 1:CHUNK, :] if d == 0 else acum_e[0:1, :]
        cb = _dot_nt(cm_b, bm_b)
        xdt = xs_c * dt_e
        xdtd = (xdt * jnp.exp(last_e - acum_e)).astype(BF16)
        ht = ht_s[d]
        y_off = _dot(cm_b, ht.astype(BF16)) * jnp.exp(acum_e)
        ht_s[d] = ht * jnp.exp(last_e) + _dot_tn(bm_b, xdtd)
        for pair in range(4):
            ra = d * 8 + 2 * pair
            lanes = slice(pair * CHUNK, (pair + 1) * CHUNK)
            blk = acum_e[:, lanes]
            swp = pltpu.roll(blk, SSM_P, 1)
            ms = []
            for r, col in ((ra, jnp.where(lane_lo, blk, swp)), (ra + 1, jnp.where(lane_lo, swp, blk))):
                seg = col - acum_r[r:r + 1, :]
                decay = jnp.exp(jnp.where(mask, seg, -jnp.inf))
                ms.append((cb * decay).astype(BF16))
            m2 = jnp.concatenate(ms, axis=1)
            xb = xdt[:, lanes].astype(BF16)
            zero = jnp.zeros_like(xb)
            xbd = jnp.concatenate([jnp.where(lane_lo, xb, zero),
                                   jnp.where(lane_lo, zero, xb)], axis=0)
            out = yf_s if d == 0 else yb_s
            out[pl.ds(r0, CHUNK), lanes] = _dot(m2, xbd) + y_off[:, lanes]

    def chunk_body(c, carry):
        direction(0, c)
        direction(1, c)
        return carry

    lax.fori_loop(0, nc, chunk_body, 0, unroll=2)

    def finish(t, carry):
        rows = pl.ds(pl.multiple_of(t * CHUNK, CHUNK), CHUNK)
        y = yf_s[rows, :] + yb_s[rows, :] + dsk_ref[...] * xs_s[rows, :]
        y_ref[rows, :] = (y * _silu(z_ref[rows, :])).astype(y_ref.dtype)
        return carry

    lax.fori_loop(0, nc, finish, 0)

    if emit_state:
        hf_ref[0] = jnp.transpose(ht_s[0])
        hb_ref[0] = jnp.transpose(ht_s[1])


def _ssd_call(zx, dt_raw, dtr, conv_w, conv_b, dtb, dtb_r, al_r, al_all, e_sel, dsk, *,
              L, nb, row_blk0, h0=None, emit_state=False, y_prev=None):
    nc = L // CHUNK
    has_h0 = h0 is not None
    rb = lambda b: row_blk0 + b
    xoff = D_INNER // GW
    boff = 2 * D_INNER // SSM_N
    coff = boff + SSM_G
    in_specs = [
        pl.BlockSpec((L, GW), lambda b, g: (rb(b), g)),
        pl.BlockSpec((L, GW), lambda b, g: (rb(b), xoff + g)),
        pl.BlockSpec((L, SSM_N), lambda b, g: (rb(b), boff + g)),
        pl.BlockSpec((L, SSM_N), lambda b, g: (rb(b), coff + g)),
        pl.BlockSpec((L, 128), lambda b, g: (rb(b), 0)),
        pl.BlockSpec((1, nc, 16, CHUNK), lambda b, g: (g, rb(b), 0, 0)),
        pl.BlockSpec((3, GW), lambda b, g: (0, g)),
        pl.BlockSpec((3, SSM_N), lambda b, g: (0, D_INNER // SSM_N + g)),
        pl.BlockSpec((3, SSM_N), lambda b, g: (0, D_INNER // SSM_N + SSM_G + g)),
        pl.BlockSpec((1, GW), lambda b, g: (0, g)),
        pl.BlockSpec((1, SSM_N), lambda b, g: (0, D_INNER // SSM_N + g)),
        pl.BlockSpec((1, SSM_N), lambda b, g: (0, D_INNER // SSM_N + SSM_G + g)),
        pl.BlockSpec((1, 128), lambda b, g: (0, 0)),
        pl.BlockSpec((1, 16, 1), lambda b, g: (g, 0, 0)),
        pl.BlockSpec((1, 16, 1), lambda b, g: (g, 0, 0)),
        pl.BlockSpec((1, 128), lambda b, g: (0, 0)),
        pl.BlockSpec((1, 2, 256, GW), lambda b, g: (g, 0, 0, 0)),
        pl.BlockSpec((1, GW), lambda b, g: (0, g)),
    ]
    args = [zx, zx, zx, zx, dt_raw, dtr, conv_w, conv_w, conv_w, conv_b, conv_b, conv_b,
            dtb, dtb_r, al_r, al_all, e_sel, dsk]
    st_spec = pl.BlockSpec((1, GW, SSM_N), lambda b, g: (b, g, 0))
    if has_h0:
        in_specs += [st_spec, st_spec]
        args += [h0[0], h0[1]]
    out_specs = [pl.BlockSpec((L, GW), lambda b, g: (rb(b), g))]
    out_shape = [jax.ShapeDtypeStruct((NT, D_INNER), BF16)]
    if emit_state:
        out_specs += [st_spec, st_spec]
        out_shape += [jax.ShapeDtypeStruct((nb, D_INNER, SSM_N), F32)] * 2
    body = functools.partial(_ssd_kernel, L=L, has_h0=has_h0, emit_state=emit_state)
    aliases = {}
    if y_prev is not None:
        in_specs.append(pl.BlockSpec(memory_space=pl.ANY))
        args.append(y_prev)
        aliases = {len(args) - 1: 0}
        body = _drop_ref(body, len(args) - 1)
    return pl.pallas_call(
        body,
        grid=(nb, SSM_G),
        in_specs=in_specs,
        out_specs=out_specs,
        out_shape=out_shape,
        scratch_shapes=[pltpu.VMEM((L, GW), F32), pltpu.VMEM((L, SSM_N), F32),
                        pltpu.VMEM((L, SSM_N), F32), pltpu.VMEM((L, 128), F32),
                        pltpu.VMEM((L, GW), F32), pltpu.VMEM((L, GW), F32),
                        pltpu.VMEM((2, SSM_N, GW), F32),
                        pltpu.VMEM((2, L, GW), F32), pltpu.VMEM((2, L, GW), F32),
                        pltpu.VMEM((2, L // CHUNK, 16, CHUNK), F32)],
        input_output_aliases=aliases,
        compiler_params=_cparams(("parallel", "parallel")),
        name="ssd",
    )(*args)


def _ssm_mixer(zx, dt_raw, h0_f, h0_b, conv_w, conv_b, dt_bias, a_log, d_skip):
    dt_g = dt_raw.reshape(NT, 2, SSM_G, 8).transpose(2, 0, 1, 3).reshape(SSM_G, NT, 16)
    dtr = dt_g.reshape(SSM_G, NT // CHUNK, CHUNK, 16).transpose(0, 1, 3, 2)

    def per_group(p):
        return p.reshape(2, SSM_G, 8).transpose(1, 0, 2).reshape(SSM_G, 16, 1)

    src = jnp.arange(2 * SSM_HEADS)[None, None, :, None]
    dst = (jnp.arange(2)[None, :, None, None] * SSM_HEADS + jnp.arange(SSM_G)[:, None, None, None] * 8
           + jnp.arange(GW)[None, None, None, :] // SSM_P)
    e_sel = (src == dst).astype(BF16)
    e_sel = jnp.concatenate([e_sel, e_sel], axis=2)
    dsk = jnp.repeat(d_skip, SSM_P).reshape(1, D_INNER)
    common = (zx, dt_raw, dtr, conv_w, conv_b.reshape(1, CONV_CH), dt_bias.reshape(1, 2 * SSM_HEADS),
              per_group(dt_bias), per_group(a_log), a_log.reshape(1, 2 * SSM_HEADS), e_sel, dsk)
    y, hf, hb = _ssd_call(*common, L=L_P, nb=NB_P, row_blk0=0, emit_state=True)
    h0 = (h0_f.reshape(NB_S, D_INNER, SSM_N), h0_b.reshape(NB_S, D_INNER, SSM_N))
    (y,) = _ssd_call(*common, L=L_S, nb=NB_S, row_blk0=NP // L_S, h0=h0, y_prev=y)
    st = (NB_P, SSM_HEADS, SSM_P, SSM_N)
    return y, hf.reshape(st), hb.reshape(st)


def _rope_tables():
    rows = L_S // GRID_W
    rowp = jnp.repeat(jnp.arange(rows, dtype=F32), GRID_W)
    colp = jnp.tile(jnp.arange(GRID_W, dtype=F32), rows)
    nf = DH // 4
    inv = ROPE_BASE ** (-jnp.arange(nf, dtype=F32) / nf)
    a0, a1 = rowp[:, None] * inv, colp[:, None] * inv
    c0, c1, s0, s1 = jnp.cos(a0), jnp.cos(a1), jnp.sin(a0), jnp.sin(a1)
    zz = jnp.zeros_like(s0)
    cos = jnp.concatenate([c0, c0, c1, c1], axis=-1)
    sin_next = jnp.concatenate([-s0, zz, -s1, zz], axis=-1)
    sin_prev = jnp.concatenate([zz, s0, zz, s1], axis=-1)
    return cos, sin_next, sin_prev


def _rope(x, cos, sin_next, sin_prev):
    return x * cos + pltpu.roll(x, 96, 1) * sin_next + pltpu.roll(x, 32, 1) * sin_prev


def _diff_kernel(*refs, L, latent, lambda_init, hb):
    q0_ref, q1_ref, k0_ref, k1_ref, v_ref = refs[:5]
    pos = 5
    if latent:
        ck0_ref, ck1_ref, cv_ref, cq_ref, snq_ref, spq_ref, ck_ref, snk_ref, spk_ref = refs[pos:pos + 9]
        pos += 9
    lam_ref, g_ref, o_ref = refs[pos:pos + 3]
    pos += 3
    if not latent:
        kout_ref, vout_ref = refs[pos:pos + 2]
        pos += 2
    kbuf, vbuf = refs[pos:pos + 2]
    off = PAST if latent else 0

    @pl.when(pl.program_id(2) == 0)
    def _():
        if not latent:
            kout_ref[:, 0:hb * DH] = k0_ref[...]
            kout_ref[:, hb * DH:2 * hb * DH] = k1_ref[...]
            vout_ref[...] = v_ref[...]
        for hh in range(hb):
            c1 = slice(hh * DH, (hh + 1) * DH)
            c2 = slice(hh * 2 * DH, (hh + 1) * 2 * DH)
            if latent:
                kbuf[0, hh, 0:PAST, :] = ck0_ref[0, :, c1].astype(BF16)
                kbuf[1, hh, 0:PAST, :] = ck1_ref[0, :, c1].astype(BF16)
                vbuf[hh, 0:PAST, :] = cv_ref[0, :, c2].astype(BF16)
            for m, k_ref in enumerate((k0_ref, k1_ref)):
                k = k_ref[:, c1]
                if latent:
                    k = _rope(k, ck_ref[...], snk_ref[...], spk_ref[...])
                kbuf[m, hh, off:off + L, :] = k.astype(BF16)
            vbuf[hh, off:off + L, :] = v_ref[:, c2].astype(BF16)

    lv = lam_ref[...]
    lam = (jnp.exp(jnp.sum(lv[0:1] * lv[1:2], axis=1, keepdims=True))
           - jnp.exp(jnp.sum(lv[2:3] * lv[3:4], axis=1, keepdims=True)) + lambda_init)
    for hh in range(hb):
        c1 = slice(hh * DH, (hh + 1) * DH)
        c2 = slice(hh * 2 * DH, (hh + 1) * 2 * DH)
        ps = []
        for m, q_ref in enumerate((q0_ref, q1_ref)):
            q = q_ref[:, c1]
            if latent:
                q = _rope(q, cq_ref[...], snq_ref[...], spq_ref[...])
            s = _dot_nt(q.astype(BF16), kbuf[m, hh]) * (DH ** -0.5)
            e = jnp.exp(s - jnp.max(s, axis=-1, keepdims=True))
            ps.append(e / jnp.sum(e, axis=-1, keepdims=True))
        p = (ps[0] - lam * ps[1]).astype(BF16)
        o = _dot(p, vbuf[hh])
        o = o * lax.rsqrt(jnp.mean(o * o, axis=-1, keepdims=True) + EPS) * g_ref[...]
        o_ref[:, c2] = (o * (1.0 - lambda_init)).astype(BF16)


def _diff_call(qkv, lam_p, subln_g, lambda_init, *, L, nb, row_blk0, qb, hb, cache=None, tables=None,
               o_prev=None):
    latent = cache is not None
    nq = L // qb
    nh = DIFF_H // hb
    kt = L + (PAST if latent else 0)
    rq = lambda b, qi: (row_blk0 + b) * nq + qi
    rs = lambda b: row_blk0 + b
    in_specs = [
        pl.BlockSpec((qb, hb * DH), lambda b, h, qi: (rq(b, qi), h)),
        pl.BlockSpec((qb, hb * DH), lambda b, h, qi: (rq(b, qi), nh + h)),
        pl.BlockSpec((L, hb * DH), lambda b, h, qi: (rs(b), 2 * nh + h)),
        pl.BlockSpec((L, hb * DH), lambda b, h, qi: (rs(b), 3 * nh + h)),
        pl.BlockSpec((L, hb * 2 * DH), lambda b, h, qi: (rs(b), 2 * nh + h)),
    ]
    args = [qkv] * 5
    if latent:
        ck, cv = cache
        in_specs += [
            pl.BlockSpec((1, PAST, hb * DH), lambda b, h, qi: (b, 0, h)),
            pl.BlockSpec((1, PAST, hb * DH), lambda b, h, qi: (b, 0, nh + h)),
            pl.BlockSpec((1, PAST, hb * 2 * DH), lambda b, h, qi: (b, 0, h)),
        ]
        args += [ck, ck, cv]
        in_specs += [pl.BlockSpec((qb, DH), lambda b, h, qi: (qi, 0))] * 3
        in_specs += [pl.BlockSpec((L, DH), lambda b, h, qi: (0, 0))] * 3
        args += list(tables) * 2
    in_specs += [pl.BlockSpec((4, DH), lambda b, h, qi: (0, 0)),
                 pl.BlockSpec((1, 2 * DH), lambda b, h, qi: (0, 0))]
    args += [lam_p, subln_g.reshape(1, 2 * DH)]
    body = functools.partial(_diff_kernel, L=L, latent=latent, lambda_init=lambda_init, hb=hb)
    aliases = {}
    if o_prev is not None:
        in_specs.append(pl.BlockSpec(memory_space=pl.ANY))
        args.append(o_prev)
        aliases = {len(args) - 1: 0}
        body = _drop_ref(body, len(args) - 1)
    out_specs = [pl.BlockSpec((qb, hb * 2 * DH), lambda b, h, qi: (rq(b, qi), h))]
    out_shape = [jax.ShapeDtypeStruct((NT, D), BF16)]
    if not latent:
        assert hb == DIFF_H and nq == 1 and row_blk0 == 0
        out_specs += [pl.BlockSpec((L, D), lambda b, h, qi: (b, 0))] * 2
        out_shape += [jax.ShapeDtypeStruct((nb * L, D), F32)] * 2
    return pl.pallas_call(
        body,
        grid=(nb, nh, nq),
        in_specs=in_specs,
        out_specs=out_specs,
        out_shape=out_shape,
        scratch_shapes=[pltpu.VMEM((2, hb, kt, DH), BF16), pltpu.VMEM((hb, kt, 2 * DH), BF16)],
        input_output_aliases=aliases,
        compiler_params=_cparams(("parallel", "parallel", "arbitrary")),
        name="diff_attn",
    )(*args)


def _win_kernel(*refs, L, latent, gb):
    q_ref, k_ref, v_ref = refs[:3]
    pos = 3
    if latent:
        ck_ref, cv_ref, cq_ref, snq_ref, spq_ref, ckk_ref, snk_ref, spk_ref = refs[pos:pos + 8]
        pos += 8
    sink_ref, o_ref = refs[pos:pos + 2]
    pos += 2
    if not latent:
        kout_ref, vout_ref = refs[pos:pos + 2]
        pos += 2
        kout_ref[...] = k_ref[...]
        vout_ref[...] = v_ref[...]
    g = pl.program_id(1)
    n = pl.program_id(2)
    scale = DH ** -0.5
    qpk = WIN_H // WIN_KV

    if latent:
        kbuf, vbuf = refs[pos:pos + 2]
        koff = PAST + WINDOW

        @pl.when(n == 0)
        def _():
            zpad = jnp.zeros((WINDOW, DH), BF16)
            for gg in range(gb):
                cols = slice(gg * DH, (gg + 1) * DH)
                for buf, c_ref, x_ref, rope in ((kbuf, ck_ref, k_ref, True), (vbuf, cv_ref, v_ref, False)):
                    buf[gg, 0:PAST, :] = c_ref[0, :, cols].astype(BF16)
                    buf[gg, PAST:koff, :] = zpad
                    x = x_ref[:, cols]
                    if rope:
                        x = _rope(x, ckk_ref[...], snk_ref[...], spk_ref[...])
                    buf[gg, koff:koff + L, :] = x.astype(BF16)
                    buf[gg, koff + L:koff + L + WINDOW, :] = zpad

        band0 = pl.multiple_of(PAST + n * WINDOW, WINDOW)
        qi_ = lax.broadcasted_iota(jnp.int32, (WINDOW, 3 * WINDOW), 0)
        kj_ = lax.broadcasted_iota(jnp.int32, (WINDOW, 3 * WINDOW), 1)
        kpos = (n - 1) * WINDOW + kj_
        delta = kj_ - qi_
        valid = (delta >= 0) & (delta <= 2 * WINDOW) & (kpos >= 0) & (kpos < L)

    for gg in range(gb):
        if latent:
            kc, vc = kbuf[gg, 0:PAST, :], vbuf[gg, 0:PAST, :]
            kb, vb = kbuf[gg, pl.ds(band0, 3 * WINDOW), :], vbuf[gg, pl.ds(band0, 3 * WINDOW), :]
        else:
            kc = k_ref[:, gg * DH:(gg + 1) * DH].astype(BF16)
            vc = v_ref[:, gg * DH:(gg + 1) * DH].astype(BF16)
        for r in range(qpk):
            cols = slice((gg * qpk + r) * DH, (gg * qpk + r + 1) * DH)
            q = q_ref[:, cols]
            if latent:
                q = _rope(q, cq_ref[...], snq_ref[...], spq_ref[...])
            qb_ = q.astype(BF16)
            sink = sink_ref[(g * gb + gg) * qpk + r]
            s_c = _dot_nt(qb_, kc) * scale
            mx = jnp.maximum(jnp.max(s_c, axis=-1, keepdims=True), sink)
            if latent:
                s_b = jnp.where(valid, _dot_nt(qb_, kb) * scale, -jnp.inf)
                mx = jnp.maximum(mx, jnp.max(s_b, axis=-1, keepdims=True))
            e_c = jnp.exp(s_c - mx)
            den = jnp.sum(e_c, axis=-1, keepdims=True) + jnp.exp(sink - mx)
            if latent:
                e_b = jnp.exp(s_b - mx)
                den = den + jnp.sum(e_b, axis=-1, keepdims=True)
            o = _dot((e_c / den).astype(BF16), vc)
            if latent:
                o = o + _dot((e_b / den).astype(BF16), vb)
            o_ref[:, cols] = o.astype(BF16)


def _win_call(qkv, sink, *, L, nb, row_blk0, qb, gb, cache=None, tables=None, o_prev=None):
    latent = cache is not None
    nq = L // qb
    ng = WIN_KV // gb
    qw = gb * (WIN_H // WIN_KV) * DH
    koff = WIN_H // gb
    voff = (WIN_H + WIN_KV) // gb
    rq = lambda b, qi: (row_blk0 + b) * nq + qi
    rs = lambda b: row_blk0 + b
    in_specs = [
        pl.BlockSpec((qb, qw), lambda b, g, qi: (rq(b, qi), g)),
        pl.BlockSpec((L, gb * DH), lambda b, g, qi: (rs(b), koff + g)),
        pl.BlockSpec((L, gb * DH), lambda b, g, qi: (rs(b), voff + g)),
    ]
    args = [qkv] * 3
    scratch = []
    if latent:
        ck, cv = cache
        in_specs += [pl.BlockSpec((1, PAST, gb * DH), lambda b, g, qi: (b, 0, g))] * 2
        args += [ck, cv]
        in_specs += [pl.BlockSpec((qb, DH), lambda b, g, qi: (qi, 0))] * 3
        in_specs += [pl.BlockSpec((L, DH), lambda b, g, qi: (0, 0))] * 3
        args += list(tables) * 2
        kt = PAST + L + 2 * WINDOW
        scratch = [pltpu.VMEM((gb, kt, DH), BF16), pltpu.VMEM((gb, kt, DH), BF16)]
    in_specs.append(pl.BlockSpec(memory_space=pltpu.SMEM))
    args.append(sink)
    body = functools.partial(_win_kernel, L=L, latent=latent, gb=gb)
    aliases = {}
    if o_prev is not None:
        in_specs.append(pl.BlockSpec(memory_space=pl.ANY))
        args.append(o_prev)
        aliases = {len(args) - 1: 0}
        body = _drop_ref(body, len(args) - 1)
    out_specs = [pl.BlockSpec((qb, qw), lambda b, g, qi: (rq(b, qi), g))]
    out_shape = [jax.ShapeDtypeStruct((NT, D), BF16)]
    if not latent:
        assert gb == WIN_KV and nq == 1 and row_blk0 == 0
        out_specs += [pl.BlockSpec((L, gb * DH), lambda b, g, qi: (b, 0))] * 2
        out_shape += [jax.ShapeDtypeStruct((nb * L, gb * DH), F32)] * 2
    return pl.pallas_call(
        body,
        grid=(nb, ng, nq),
        in_specs=in_specs,
        out_specs=out_specs,
        out_shape=out_shape,
        scratch_shapes=scratch,
        input_output_aliases=aliases,
        compiler_params=_cparams(("parallel", "parallel", "arbitrary")),
        name="win_attn",
    )(*args)


def kernel(x_prompt, x_sample, state_l0_fwd, state_l0_bwd, cache_l1_k, cache_l1_v, cache_l2_k, cache_l2_v, state_l3_fwd, state_l3_bwd, c, c_ctx, norm_g, w_ada, b_ada, ffn1_w_in, ffn1_w_out, ffn2_w_in, ffn2_w_out, ssm_w_in, ssm_conv_w, ssm_conv_b, ssm_dt_bias, ssm_a_log, ssm_d, ssm_norm_g, ssm_w_out, diff_w_qkv, diff_lambda, diff_subln_g, diff_w_out, win_w_qkv, win_sink, win_w_out, final_norm_g):
    x = _stack_tokens(x_prompt.reshape(NP, D), x_sample.reshape(NS, D))
    cv8 = jnp.concatenate([c_ctx[None, :], c, jnp.zeros((8 - 1 - NB_S, D), F32)], axis=0)
    mods = _modulation(cv8, w_ada, b_ada)
    norm_g4 = norm_g.reshape(DEPTH, 3, 1, D)
    tables = _rope_tables()
    ssm_states = [(state_l0_fwd, state_l0_bwd), (state_l3_fwd, state_l3_bwd)]
    new_state = []
    for l in range(DEPTH):
        m, j = l % 3, l // 3
        x = _ffn(x, mods, norm_g4, ffn1_w_in, ffn1_w_out, l, 0)
        if m == 0:
            zx, dt_raw = _inproj(x, mods, norm_g4, ssm_w_in, l, j, 1024, n_main=D_INNER + CONV_CH)
            y, hf, hb = _ssm_mixer(zx, dt_raw, *ssm_states[j], ssm_conv_w[j], ssm_conv_b[j],
                                   ssm_dt_bias[j], ssm_a_log[j], ssm_d[j])
            x = _outproj(y, ssm_w_out, j, x, mods, l, norm_g=ssm_norm_g[j])
            new_state += [hf, hb]
        elif m == 1:
            lambda_init = 0.8 - 0.6 * math.exp(-0.3 * l)
            (qkv,) = _inproj(x, mods, norm_g4, diff_w_qkv, l, j, 512)
            o, k_new, v_new = _diff_call(qkv, diff_lambda[j], diff_subln_g[j], lambda_init,
                                         L=L_P, nb=NB_P, row_blk0=0, qb=L_P, hb=DIFF_H)
            cache = (cache_l1_k.reshape(NB_S, PAST, D), cache_l1_v.reshape(NB_S, PAST, D))
            (o,) = _diff_call(qkv, diff_lambda[j], diff_subln_g[j], lambda_init,
                              L=L_S, nb=NB_S, row_blk0=NP // L_S, qb=256, hb=4, cache=cache,
                              tables=tables, o_prev=o)
            x = _outproj(o, diff_w_out, j, x, mods, l)
            new_state += [k_new.reshape(NB_P, L_P, 2, DIFF_H, DH),
                          v_new.reshape(NB_P, L_P, DIFF_H, 2 * DH)]
        else:
            (qkv,) = _inproj(x, mods, norm_g4, win_w_qkv, l, j, 512)
            o, k_new, v_new = _win_call(qkv, win_sink[j], L=L_P, nb=NB_P, row_blk0=0, qb=L_P,
                                        gb=WIN_KV)
            kvw = WIN_KV * DH
            cache = (cache_l2_k.reshape(NB_S, PAST, kvw), cache_l2_v.reshape(NB_S, PAST, kvw))
            (o,) = _win_call(qkv, win_sink[j], L=L_S, nb=NB_S, row_blk0=NP // L_S, qb=WINDOW, gb=4,
                             cache=cache, tables=tables, o_prev=o)
            x = _outproj(o, win_w_out, j, x, mods, l)
            new_state += [k_new.reshape(NB_P, L_P, WIN_KV, DH), v_new.reshape(NB_P, L_P, WIN_KV, DH)]
        x = _ffn(x, mods, norm_g4, ffn2_w_in, ffn2_w_out, l, 2)
    y_p = _final_norm(x, final_norm_g, 0, NP)
    y_s = _final_norm(x, final_norm_g, NP, NS)
    s0f, s0b, k1, v1, k2, v2, s3f, s3b = new_state
    return (y_p.reshape(NB_P, L_P, D), y_s.reshape(NB_S, L_S, D),
            s0f, s0b, k1, v1, k2, v2, s3f, s3b)
```

```python
import functools
import math

import jax
import jax.numpy as jnp
from jax import lax
from jax.experimental import pallas as pl
from jax.experimental.pallas import tpu as pltpu

F32 = jnp.float32
BF16 = jnp.bfloat16

D = 2048
NB_P, L_P = 16, 256
NB_S, L_S = 2, 1024
PAST = 256
NP = NB_P * L_P
NS = NB_S * L_S
NT = NP + NS
DEPTH = 4
N_MOD = 9
D_FF = 5632
EPS = 1e-6
GRID_W = 64
ROPE_BASE = 10000.0

D_INNER = 4096
SSM_HEADS = 64
SSM_P = 64
SSM_G = 8
SSM_N = 128
CONV_CH = D_INNER + 2 * SSM_G * SSM_N
SSM_IN = D_INNER + CONV_CH + 2 * SSM_HEADS
CHUNK = 128
GW = D_INNER // SSM_G

DIFF_H = 8
DH = 128
WIN_H = 16
WIN_KV = 4
WINDOW = 128

TM = 1024
TM_WIDE = 2 * TM
ROW_CHUNK = 256
STACK_ROWS = 512
VMEM_LIMIT = 60 * 1024 * 1024


def _resident_spec(block_shape, index_map):
    return pl.BlockSpec(block_shape, index_map, pipeline_mode=pl.Buffered(1))


def _cparams(sem):
    return pltpu.CompilerParams(dimension_semantics=sem, vmem_limit_bytes=VMEM_LIMIT)


def _drop_ref(kernel_fn, idx):
    def body(*refs):
        return kernel_fn(*(refs[:idx] + refs[idx + 1:]))
    return body


def _silu(x):
    return x * (0.5 * jnp.tanh(0.5 * x) + 0.5)


def _dot(a, b):
    return jnp.dot(a, b, preferred_element_type=F32)


def _dot_nt(a, b):
    return lax.dot_general(a, b, (((1,), (1,)), ((), ())), preferred_element_type=F32)


def _dot_tn(a, b):
    return lax.dot_general(a, b, (((0,), (0,)), ((), ())), preferred_element_type=F32)


def _seg_rows(start):
    return jnp.where(start < NP, 0, 1 + (start - NP) // L_S)


def _seg(i, tm):
    return _seg_rows(i * tm)


def _modnorm(x, g, shift, scale):
    y = x * lax.rsqrt(jnp.mean(x * x, axis=-1, keepdims=True) + EPS) * g
    return y * (1.0 + scale) + shift


def _stack_kernel(xp_ref, xs_ref, o_ref):
    i = pl.program_id(0)

    @pl.when(i < NP // STACK_ROWS)
    def _():
        o_ref[...] = xp_ref[...]

    @pl.when(i >= NP // STACK_ROWS)
    def _():
        o_ref[...] = xs_ref[...]


def _stack_tokens(xp, xs):
    npb = NP // STACK_ROWS
    return pl.pallas_call(
        _stack_kernel,
        grid=(NT // STACK_ROWS,),
        in_specs=[pl.BlockSpec((STACK_ROWS, D), lambda i: (jnp.minimum(i, npb - 1), 0)),
                  pl.BlockSpec((STACK_ROWS, D), lambda i: (jnp.maximum(i - npb, 0), 0))],
        out_specs=pl.BlockSpec((STACK_ROWS, D), lambda i: (i, 0)),
        out_shape=jax.ShapeDtypeStruct((NT, D), F32),
        compiler_params=_cparams(("arbitrary",)),
        name="stack_tokens",
    )(xp, xs)


def _mod_kernel(cv_ref, w_ref, b_ref, o_ref):
    @pl.when(pl.program_id(2) == 0)
    def _():
        o_ref[0, 0] = jnp.broadcast_to(b_ref[0, 0], (8, D))

    cv = cv_ref[...]
    o_ref[0, 0] += _dot(_silu(cv).astype(BF16), w_ref[0].astype(BF16))


def _modulation(cv8, w_ada, b_ada):
    kc = 1024
    out = pl.pallas_call(
        _mod_kernel,
        grid=(DEPTH, N_MOD, D // kc),
        in_specs=[
            pl.BlockSpec((8, kc), lambda l, j, k: (0, k)),
            pl.BlockSpec((1, kc, D), lambda l, j, k: (l, k, j)),
            pl.BlockSpec((1, 1, 1, D), lambda l, j, k: (l, j, 0, 0)),
        ],
        out_specs=pl.BlockSpec((1, 1, 8, D), lambda l, j, k: (l, j, 0, 0)),
        out_shape=jax.ShapeDtypeStruct((DEPTH, N_MOD, 8, D), F32),
        compiler_params=_cparams(("parallel", "parallel", "arbitrary")),
        name="modulation",
    )(cv8, w_ada, b_ada.reshape(DEPTH, N_MOD, 1, D))
    return out.reshape(DEPTH, N_MOD, 8, 1, D)


def _mod_spec(l, k):
    return pl.BlockSpec((1, 1, 1, 1, D), lambda i, j: (l, k, _seg(i, TM), 0, 0))


def _mod_spec_wide(l, k, half):
    return pl.BlockSpec((1, 1, 1, 1, D), lambda i, j: (l, k, _seg_rows(i * TM_WIDE + half * TM), 0, 0))


def _ffn_kernel(x_ref, sh_ref, sc_ref, gt_ref, g_ref, wg_ref, wu_ref, wo_ref, o_ref, h_scr):
    j = pl.program_id(1)

    def weights():
        return wg_ref[0].astype(BF16), wu_ref[0].astype(BF16), wo_ref[0].astype(BF16)

    def hidden_chunk(h, wg, wu, wo):
        act = (_silu(_dot(h, wg)) * _dot(h, wu)).astype(BF16)
        return _dot(act, wo)

    @pl.when(j == 0)
    def _():
        w = weights()
        for r in range(TM // ROW_CHUNK):
            rows = pl.ds(r * ROW_CHUNK, ROW_CHUNK)
            h = _modnorm(x_ref[rows, :], g_ref[0, 0], sh_ref[0, 0, 0], sc_ref[0, 0, 0]).astype(BF16)
            h_scr[rows, :] = h
            o_ref[rows, :] = hidden_chunk(h, *w)

    last = pl.num_programs(1) - 1

    @pl.when(jnp.logical_and(j > 0, j < last))
    def _():
        o_ref[...] += hidden_chunk(h_scr[...], *weights())

    @pl.when(j == last)
    def _():
        ffn = o_ref[...] + hidden_chunk(h_scr[...], *weights())
        o_ref[...] = x_ref[...] + 0.5 * gt_ref[0, 0, 0] * ffn


def _ffn(x, mods, norm_g4, w_in, w_out, l, sub):
    tf = 256
    nf = D_FF // tf
    return pl.pallas_call(
        _ffn_kernel,
        grid=(NT // TM, nf),
        in_specs=[
            pl.BlockSpec((TM, D), lambda i, j: (i, 0)),
            _mod_spec(l, 3 * sub), _mod_spec(l, 3 * sub + 1), _mod_spec(l, 3 * sub + 2),
            pl.BlockSpec((1, 1, 1, D), lambda i, j: (l, sub, 0, 0)),
            pl.BlockSpec((1, D, tf), lambda i, j: (l, 0, j)),
            pl.BlockSpec((1, D, tf), lambda i, j: (l, 0, j + nf)),
            pl.BlockSpec((1, tf, D), lambda i, j: (l, j, 0)),
        ],
        out_specs=pl.BlockSpec((TM, D), lambda i, j: (i, 0)),
        out_shape=jax.ShapeDtypeStruct((NT, D), F32),
        scratch_shapes=[pltpu.VMEM((TM, D), BF16)],
        compiler_params=_cparams(("parallel", "arbitrary")),
        name="ffn",
    )(x, mods, mods, mods, norm_g4, w_in, w_in, w_out)


def _inproj_kernel(*refs, tail):
    if tail:
        x_ref, sh0_ref, sh1_ref, sc0_ref, sc1_ref, g_ref, w_ref, wt_ref, o_ref, ot_ref, h_scr = refs
    else:
        x_ref, sh0_ref, sh1_ref, sc0_ref, sc1_ref, g_ref, w_ref, o_ref, h_scr = refs
    j = pl.program_id(1)

    @pl.when(j == 0)
    def _():
        w = w_ref[0].astype(BF16)
        if tail:
            wt = wt_ref[0].astype(BF16)
        for r in range(TM_WIDE // ROW_CHUNK):
            rows = pl.ds(r * ROW_CHUNK, ROW_CHUNK)
            sh_ref, sc_ref = ((sh0_ref, sc0_ref), (sh1_ref, sc1_ref))[r * ROW_CHUNK // TM]
            h = _modnorm(x_ref[rows, :], g_ref[0, 0], sh_ref[0, 0, 0], sc_ref[0, 0, 0]).astype(BF16)
            h_scr[rows, :] = h
            o_ref[rows, :] = _dot(h, w)
            if tail:
                ot_ref[rows, :] = _dot(h, wt)

    @pl.when(j > 0)
    def _():
        o_ref[...] = _dot(h_scr[...], w_ref[0].astype(BF16))


def _inproj(x, mods, norm_g4, w, l, j_w, tn, n_main=None):
    n = w.shape[-1]
    n_main = n if n_main is None else n_main
    n_tail = n - n_main
    in_specs = [
        _resident_spec((TM_WIDE, D), lambda i, j: (i, 0)),
        _mod_spec_wide(l, 3, 0), _mod_spec_wide(l, 3, 1),
        _mod_spec_wide(l, 4, 0), _mod_spec_wide(l, 4, 1),
        pl.BlockSpec((1, 1, 1, D), lambda i, j: (l, 1, 0, 0)),
        pl.BlockSpec((1, D, tn), lambda i, j: (j_w, 0, j)),
    ]
    args = [x, mods, mods, mods, mods, norm_g4, w]
    out_specs = [pl.BlockSpec((TM_WIDE, tn), lambda i, j: (i, j))]
    out_shape = [jax.ShapeDtypeStruct((NT, n_main), F32)]
    if n_tail:
        in_specs.append(pl.BlockSpec((1, D, n_tail), lambda i, j: (j_w, 0, n_main // n_tail)))
        args.append(w)
        out_specs.append(pl.BlockSpec((TM_WIDE, n_tail), lambda i, j: (i, 0)))
        out_shape.append(jax.ShapeDtypeStruct((NT, n_tail), F32))
    return pl.pallas_call(
        functools.partial(_inproj_kernel, tail=bool(n_tail)),
        grid=(NT // TM_WIDE, n_main // tn),
        in_specs=in_specs,
        out_specs=out_specs,
        out_shape=out_shape,
        scratch_shapes=[pltpu.VMEM((TM_WIDE, D), BF16)],
        compiler_params=_cparams(("parallel", "arbitrary")),
        name="inproj",
    )(*args)


def _outproj_kernel(*refs, norm):
    if norm:
        y_ref, ng_ref, w_ref, x_ref, gt_ref, o_ref, yn_scr = refs
        j = pl.program_id(1)

        @pl.when(j == 0)
        def _():
            w = w_ref[0].astype(BF16)
            for r in range(TM // ROW_CHUNK):
                rows = pl.ds(r * ROW_CHUNK, ROW_CHUNK)
                y = y_ref[rows, :].astype(F32)
                yn = y * lax.rsqrt(jnp.mean(y * y, axis=-1, keepdims=True) + EPS) * ng_ref[...]
                yn = yn.astype(BF16)
                yn_scr[rows, :] = yn
                o_ref[rows, :] = x_ref[rows, :] + gt_ref[0, 0, 0] * _dot(yn, w)

        @pl.when(j > 0)
        def _():
            o_ref[...] = x_ref[...] + gt_ref[0, 0, 0] * _dot(yn_scr[...], w_ref[0].astype(BF16))
    else:
        y_ref, w_ref, x_ref, gt0_ref, gt1_ref, o_ref = refs
        w = w_ref[0].astype(BF16)
        for half, gt_ref in enumerate((gt0_ref, gt1_ref)):
            rows = pl.ds(half * TM, TM)
            o_ref[rows, :] = x_ref[rows, :] + gt_ref[0, 0, 0] * _dot(y_ref[rows, :], w)


def _outproj(y, w, j_w, x, mods, l, norm_g=None):
    tn = 512
    k = y.shape[-1]
    norm = norm_g is not None
    tm = TM if norm else TM_WIDE
    in_specs = [pl.BlockSpec((tm, k), lambda i, j: (i, 0))]
    args = [y]
    scratch = []
    if norm:
        in_specs.append(pl.BlockSpec((1, k), lambda i, j: (0, 0)))
        args.append(norm_g.reshape(1, k))
        scratch.append(pltpu.VMEM((tm, k), BF16))
    in_specs += [
        pl.BlockSpec((1, k, tn), lambda i, j: (j_w, 0, j)),
        pl.BlockSpec((tm, tn), lambda i, j: (i, j)),
    ]
    args += [w, x]
    for half in range(tm // TM):
        in_specs.append(pl.BlockSpec((1, 1, 1, 1, tn),
                                     lambda i, j, half=half: (l, 5, _seg_rows(i * tm + half * TM), 0, j)))
        args.append(mods)
    return pl.pallas_call(
        functools.partial(_outproj_kernel, norm=norm),
        grid=(NT // tm, D // tn),
        in_specs=in_specs,
        out_specs=pl.BlockSpec((tm, tn), lambda i, j: (i, j)),
        out_shape=jax.ShapeDtypeStruct((NT, D), F32),
        scratch_shapes=scratch,
        compiler_params=_cparams(("parallel", "arbitrary")),
        name="outproj",
    )(*args)


def _rmsnorm_kernel(x_ref, g_ref, o_ref):
    x = x_ref[...]
    o_ref[...] = x * lax.rsqrt(jnp.mean(x * x, axis=-1, keepdims=True) + EPS) * g_ref[...]


def _final_norm(x, g, row0, nrows):
    blk0 = row0 // TM
    return pl.pallas_call(
        _rmsnorm_kernel,
        grid=(nrows // TM,),
        in_specs=[pl.BlockSpec((TM, D), lambda i: (blk0 + i, 0)),
                  pl.BlockSpec((1, D), lambda i: (0, 0))],
        out_specs=pl.BlockSpec((TM, D), lambda i: (i, 0)),
        out_shape=jax.ShapeDtypeStruct((nrows, D), F32),
        compiler_params=_cparams(("parallel",)),
        name="final_norm",
    )(x, g.reshape(1, D))


def _split3(x):
    hi = x.astype(BF16)
    r = x - hi.astype(F32)
    mid = r.astype(BF16)
    lo = (r - mid.astype(F32)).astype(BF16)
    return hi, mid, lo


def _softplus(x):
    return jnp.maximum(x, 0.0) + jnp.log1p(jnp.exp(-jnp.abs(x)))


def _ssd_kernel(*refs, L, has_h0, emit_state):
    (z_ref, xs_ref, bm_ref, cm_ref, dt_ref, dtr_ref, cwx_ref, cwb_ref, cwc_ref,
     cbx_ref, cbb_ref, cbc_ref, dtb_ref, dtbr_ref, alr_ref, ala_ref, e_ref, dsk_ref) = refs[:18]
    pos = 18
    if has_h0:
        h0f_ref, h0b_ref = refs[pos:pos + 2]
        pos += 2
    y_ref = refs[pos]
    pos += 1
    if emit_state:
        hf_ref, hb_ref = refs[pos:pos + 2]
        pos += 2
    xs_s, bm_s, cm_s, dt_s, yf_s, yb_s, ht_s, dte_s, ace_s, acr_s = refs[pos:pos + 10]
    nc = L // CHUNK
    rowc = lax.broadcasted_iota(jnp.int32, (CHUNK, 1), 0)
    a_all = -jnp.exp(ala_ref[...])
    a_r = -jnp.exp(alr_ref[0])
    dtb_r = dtbr_ref[0]
    ii = lax.broadcasted_iota(jnp.int32, (CHUNK, CHUNK), 0)
    jj = lax.broadcasted_iota(jnp.int32, (CHUNK, CHUNK), 1)
    lower = ii >= jj
    upper = ii <= jj
    lane_lo = jj < SSM_P
    lower_k = jnp.concatenate([lower.astype(BF16)] * 3, axis=1)
    upper_k = jnp.concatenate([upper.astype(BF16)] * 3, axis=1)
    lower_r = jnp.concatenate([lower.astype(BF16)] * 3, axis=0)
    upper_r = jnp.concatenate([upper.astype(BF16)] * 3, axis=0)

    def prep(t, carry):
        r0 = pl.multiple_of(t * CHUNK, CHUNK)
        rows = pl.ds(r0, CHUNK)
        before = pl.ds(pl.multiple_of(jnp.maximum(r0 - 8, 0), 8), 8)
        after = pl.ds(pl.multiple_of(jnp.minimum(r0 + CHUNK, L - 8), 8), 8)
        for x_ref, w_ref, b_ref, out_s in ((xs_ref, cwx_ref, cbx_ref, xs_s),
                                           (bm_ref, cwb_ref, cbb_ref, bm_s),
                                           (cm_ref, cwc_ref, cbc_ref, cm_s)):
            x = x_ref[rows, :]
            prev = jnp.where(t == 0, 0.0, x_ref[before, :][7:8])
            nxt = jnp.where(t == nc - 1, 0.0, x_ref[after, :][0:1])
            xm = jnp.where(rowc == 0, prev, pltpu.roll(x, 1, 0))
            xp = jnp.where(rowc == CHUNK - 1, nxt, pltpu.roll(x, CHUNK - 1, 0))
            w = w_ref[...]
            out_s[rows, :] = _silu(w[0:1] * xm + w[1:2] * x + w[2:3] * xp + b_ref[...])
        dt_s[rows, :] = _softplus(dt_ref[rows, :] + dtb_ref[...])
        return carry

    lax.fori_loop(0, nc, prep, 0)

    def decays(t, carry):
        rows = pl.ds(pl.multiple_of(t * CHUNK, CHUNK), CHUNK)
        dt_c = dt_s[rows, :]
        dta_c = jnp.concatenate(_split3(dt_c * a_all), axis=0)
        dt_hm = jnp.concatenate(_split3(dt_c)[:2], axis=1)
        dta_r = _softplus(dtr_ref[0, t] + dtb_r) * a_r
        dta_r = jnp.concatenate(_split3(dta_r), axis=1)
        for d in range(2):
            m_col, m_row = (lower_k, upper_r) if d == 0 else (upper_k, lower_r)
            acum_c = _dot(m_col, dta_c)
            acr_s[d, t] = _dot(dta_r, m_row)
            e_d = e_ref[0, d]
            dte_s[d, rows, :] = _dot(dt_hm, e_d)
            ace_s[d, rows, :] = _dot(jnp.concatenate(_split3(acum_c)[:2], axis=1), e_d)
        return carry

    lax.fori_loop(0, nc, decays, 0, unroll=2)

    if has_h0:
        ht_s[0] = jnp.transpose(h0f_ref[0])
        ht_s[1] = jnp.transpose(h0b_ref[0])
    else:
        ht_s[...] = jnp.zeros_like(ht_s)

    def direction(d, c):
        cc = c if d == 0 else nc - 1 - c
        r0 = pl.multiple_of(cc * CHUNK, CHUNK)
        xs_c = xs_s[pl.ds(r0, CHUNK), :]
        bm_b = bm_s[pl.ds(r0, CHUNK), :].astype(BF16)
        cm_b = cm_s[pl.ds(r0, CHUNK), :].astype(BF16)
        mask = lower if d == 0 else upper
        dt_e = dte_s[d, pl.ds(r0, CHUNK), :]
        acum_e = ace_s[d, pl.ds(r0, CHUNK), :]
        acum_r = acr_s[d, cc]
        last_e = acum_e[CHUNK - 1:CHUNK, :] if d == 0 else acum_e[0:1, :]
        cb = _dot_nt(cm_b, bm_b)
        xdt = xs_c * dt_e
        xdtd = (xdt * jnp.exp(last_e - acum_e)).astype(BF16)
        ht = ht_s[d]
        y_off = _dot(cm_b, ht.astype(BF16)) * jnp.exp(acum_e)
        ht_s[d] = ht * jnp.exp(last_e) + _dot_tn(bm_b, xdtd)
        for pair in range(4):
            ra = d * 8 + 2 * pair
            lanes = slice(pair * CHUNK, (pair + 1) * CHUNK)
            blk = acum_e[:, lanes]
            swp = pltpu.roll(blk, SSM_P, 1)
            ms = []
            for r, col in ((ra, jnp.where(lane_lo, blk, swp)), (ra + 1, jnp.where(lane_lo, swp, blk))):
                seg = col - acum_r[r:r + 1, :]
                decay = jnp.exp(jnp.where(mask, seg, -jnp.inf))
                ms.append((cb * decay).astype(BF16))
            m2 = jnp.concatenate(ms, axis=1)
            xb = xdt[:, lanes].astype(BF16)
            zero = jnp.zeros_like(xb)
            xbd = jnp.concatenate([jnp.where(lane_lo, xb, zero),
                                   jnp.where(lane_lo, zero, xb)], axis=0)
            out = yf_s if d == 0 else yb_s
            out[pl.ds(r0, CHUNK), lanes] = _dot(m2, xbd) + y_off[:, lanes]

    def chunk_body(c, carry):
        direction(0, c)
        direction(1, c)
        return carry

    lax.fori_loop(0, nc, chunk_body, 0, unroll=2)

    def finish(t, carry):
        rows = pl.ds(pl.multiple_of(t * CHUNK, CHUNK), CHUNK)
        y = yf_s[rows, :] + yb_s[rows, :] + dsk_ref[...] * xs_s[rows, :]
        y_ref[rows, :] = (y * _silu(z_ref[rows, :])).astype(y_ref.dtype)
        return carry

    lax.fori_loop(0, nc, finish, 0)

    if emit_state:
        hf_ref[0] = jnp.transpose(ht_s[0])
        hb_ref[0] = jnp.transpose(ht_s[1])


def _ssd_call(zx, dt_raw, dtr, conv_w, conv_b, dtb, dtb_r, al_r, al_all, e_sel, dsk, *,
              L, nb, row_blk0, h0=None, emit_state=False, y_prev=None):
    nc = L // CHUNK
    has_h0 = h0 is not None
    rb = lambda b: row_blk0 + b
    xoff = D_INNER // GW
    boff = 2 * D_INNER // SSM_N
    coff = boff + SSM_G
    in_specs = [
        pl.BlockSpec((L, GW), lambda b, g: (rb(b), g)),
        pl.BlockSpec((L, GW), lambda b, g: (rb(b), xoff + g)),
        pl.BlockSpec((L, SSM_N), lambda b, g: (rb(b), boff + g)),
        pl.BlockSpec((L, SSM_N), lambda b, g: (rb(b), coff + g)),
        pl.BlockSpec((L, 128), lambda b, g: (rb(b), 0)),
        pl.BlockSpec((1, nc, 16, CHUNK), lambda b, g: (g, rb(b), 0, 0)),
        pl.BlockSpec((3, GW), lambda b, g: (0, g)),
        pl.BlockSpec((3, SSM_N), lambda b, g: (0, D_INNER // SSM_N + g)),
        pl.BlockSpec((3, SSM_N), lambda b, g: (0, D_INNER // SSM_N + SSM_G + g)),
        pl.BlockSpec((1, GW), lambda b, g: (0, g)),
        pl.BlockSpec((1, SSM_N), lambda b, g: (0, D_INNER // SSM_N + g)),
        pl.BlockSpec((1, SSM_N), lambda b, g: (0, D_INNER // SSM_N + SSM_G + g)),
        pl.BlockSpec((1, 128), lambda b, g: (0, 0)),
        pl.BlockSpec((1, 16, 1), lambda b, g: (g, 0, 0)),
        pl.BlockSpec((1, 16, 1), lambda b, g: (g, 0, 0)),
        pl.BlockSpec((1, 128), lambda b, g: (0, 0)),
        pl.BlockSpec((1, 2, 256, GW), lambda b, g: (g, 0, 0, 0)),
        pl.BlockSpec((1, GW), lambda b, g: (0, g)),
    ]
    args = [zx, zx, zx, zx, dt_raw, dtr, conv_w, conv_w, conv_w, conv_b, conv_b, conv_b,
            dtb, dtb_r, al_r, al_all, e_sel, dsk]
    st_spec = pl.BlockSpec((1, GW, SSM_N), lambda b, g: (b, g, 0))
    if has_h0:
        in_specs += [st_spec, st_spec]
        args += [h0[0], h0[1]]
    out_specs = [pl.BlockSpec((L, GW), lambda b, g: (rb(b), g))]
    out_shape = [jax.ShapeDtypeStruct((NT, D_INNER), BF16)]
    if emit_state:
        out_specs += [st_spec, st_spec]
        out_shape += [jax.ShapeDtypeStruct((nb, D_INNER, SSM_N), F32)] * 2
    body = functools.partial(_ssd_kernel, L=L, has_h0=has_h0, emit_state=emit_state)
    aliases = {}
    if y_prev is not None:
        in_specs.append(pl.BlockSpec(memory_space=pl.ANY))
        args.append(y_prev)
        aliases = {len(args) - 1: 0}
        body = _drop_ref(body, len(args) - 1)
    return pl.pallas_call(
        body,
        grid=(nb, SSM_G),
        in_specs=in_specs,
        out_specs=out_specs,
        out_shape=out_shape,
        scratch_shapes=[pltpu.VMEM((L, GW), F32), pltpu.VMEM((L, SSM_N), F32),
                        pltpu.VMEM((L, SSM_N), F32), pltpu.VMEM((L, 128), F32),
                        pltpu.VMEM((L, GW), F32), pltpu.VMEM((L, GW), F32),
                        pltpu.VMEM((2, SSM_N, GW), F32),
                        pltpu.VMEM((2, L, GW), F32), pltpu.VMEM((2, L, GW), F32),
                        pltpu.VMEM((2, L // CHUNK, 16, CHUNK), F32)],
        input_output_aliases=aliases,
        compiler_params=_cparams(("parallel", "parallel")),
        name="ssd",
    )(*args)


def _ssm_mixer(zx, dt_raw, h0_f, h0_b, conv_w, conv_b, dt_bias, a_log, d_skip):
    dt_g = dt_raw.reshape(NT, 2, SSM_G, 8).transpose(2, 0, 1, 3).reshape(SSM_G, NT, 16)
    dtr = dt_g.reshape(SSM_G, NT // CHUNK, CHUNK, 16).transpose(0, 1, 3, 2)

    def per_group(p):
        return p.reshape(2, SSM_G, 8).transpose(1, 0, 2).reshape(SSM_G, 16, 1)

    src = jnp.arange(2 * SSM_HEADS)[None, None, :, None]
    dst = (jnp.arange(2)[None, :, None, None] * SSM_HEADS + jnp.arange(SSM_G)[:, None, None, None] * 8
           + jnp.arange(GW)[None, None, None, :] // SSM_P)
    e_sel = (src == dst).astype(BF16)
    e_sel = jnp.concatenate([e_sel, e_sel], axis=2)
    dsk = jnp.repeat(d_skip, SSM_P).reshape(1, D_INNER)
    common = (zx, dt_raw, dtr, conv_w, conv_b.reshape(1, CONV_CH), dt_bias.reshape(1, 2 * SSM_HEADS),
              per_group(dt_bias), per_group(a_log), a_log.reshape(1, 2 * SSM_HEADS), e_sel, dsk)
    y, hf, hb = _ssd_call(*common, L=L_P, nb=NB_P, row_blk0=0, emit_state=True)
    h0 = (h0_f.reshape(NB_S, D_INNER, SSM_N), h0_b.reshape(NB_S, D_INNER, SSM_N))
    (y,) = _ssd_call(*common, L=L_S, nb=NB_S, row_blk0=NP // L_S, h0=h0, y_prev=y)
    st = (NB_P, SSM_HEADS, SSM_P, SSM_N)
    return y, hf.reshape(st), hb.reshape(st)


def _rope_tables():
    rows = L_S // GRID_W
    rowp = jnp.repeat(jnp.arange(rows, dtype=F32), GRID_W)
    colp = jnp.tile(jnp.arange(GRID_W, dtype=F32), rows)
    nf = DH // 4
    inv = ROPE_BASE ** (-jnp.arange(nf, dtype=F32) / nf)
    a0, a1 = rowp[:, None] * inv, colp[:, None] * inv
    c0, c1, s0, s1 = jnp.cos(a0), jnp.cos(a1), jnp.sin(a0), jnp.sin(a1)
    zz = jnp.zeros_like(s0)
    cos = jnp.concatenate([c0, c0, c1, c1], axis=-1)
    sin_next = jnp.concatenate([-s0, zz, -s1, zz], axis=-1)
    sin_prev = jnp.concatenate([zz, s0, zz, s1], axis=-1)
    return cos, sin_next, sin_prev


def _rope(x, cos, sin_next, sin_prev):
    return x * cos + pltpu.roll(x, 96, 1) * sin_next + pltpu.roll(x, 32, 1) * sin_prev


def _diff_kernel(*refs, L, latent, lambda_init, hb):
    q0_ref, q1_ref, k0_ref, k1_ref, v_ref = refs[:5]
    pos = 5
    if latent:
        ck0_ref, ck1_ref, cv_ref, cq_ref, snq_ref, spq_ref, ck_ref, snk_ref, spk_ref = refs[pos:pos + 9]
        pos += 9
    lam_ref, g_ref, o_ref = refs[pos:pos + 3]
    pos += 3
    if not latent:
        kout_ref, vout_ref = refs[pos:pos + 2]
        pos += 2
    kbuf, vbuf = refs[pos:pos + 2]
    off = PAST if latent else 0

    @pl.when(pl.program_id(2) == 0)
    def _():
        if not latent:
            kout_ref[:, 0:hb * DH] = k0_ref[...]
            kout_ref[:, hb * DH:2 * hb * DH] = k1_ref[...]
            vout_ref[...] = v_ref[...]
        for hh in range(hb):
            c1 = slice(hh * DH, (hh + 1) * DH)
            c2 = slice(hh * 2 * DH, (hh + 1) * 2 * DH)
            if latent:
                kbuf[0, hh, 0:PAST, :] = ck0_ref[0, :, c1].astype(BF16)
                kbuf[1, hh, 0:PAST, :] = ck1_ref[0, :, c1].astype(BF16)
                vbuf[hh, 0:PAST, :] = cv_ref[0, :, c2].astype(BF16)
            for m, k_ref in enumerate((k0_ref, k1_ref)):
                k = k_ref[:, c1]
                if latent:
                    k = _rope(k, ck_ref[...], snk_ref[...], spk_ref[...])
                kbuf[m, hh, off:off + L, :] = k.astype(BF16)
            vbuf[hh, off:off + L, :] = v_ref[:, c2].astype(BF16)

    lv = lam_ref[...]
    lam = (jnp.exp(jnp.sum(lv[0:1] * lv[1:2], axis=1, keepdims=True))
           - jnp.exp(jnp.sum(lv[2:3] * lv[3:4], axis=1, keepdims=True)) + lambda_init)
    for hh in range(hb):
        c1 = slice(hh * DH, (hh + 1) * DH)
        c2 = slice(hh * 2 * DH, (hh + 1) * 2 * DH)
        ps = []
        for m, q_ref in enumerate((q0_ref, q1_ref)):
            q = q_ref[:, c1]
            if latent:
                q = _rope(q, cq_ref[...], snq_ref[...], spq_ref[...])
            s = _dot_nt(q.astype(BF16), kbuf[m, hh]) * (DH ** -0.5)
            e = jnp.exp(s - jnp.max(s, axis=-1, keepdims=True))
            ps.append(e / jnp.sum(e, axis=-1, keepdims=True))
        p = (ps[0] - lam * ps[1]).astype(BF16)
        o = _dot(p, vbuf[hh])
        o = o * lax.rsqrt(jnp.mean(o * o, axis=-1, keepdims=True) + EPS) * g_ref[...]
        o_ref[:, c2] = (o * (1.0 - lambda_init)).astype(BF16)


def _diff_call(qkv, lam_p, subln_g, lambda_init, *, L, nb, row_blk0, qb, hb, cache=None, tables=None,
               o_prev=None):
    latent = cache is not None
    nq = L // qb
    nh = DIFF_H // hb
    kt = L + (PAST if latent else 0)
    rq = lambda b, qi: (row_blk0 + b) * nq + qi
    rs = lambda b: row_blk0 + b
    in_specs = [
        pl.BlockSpec((qb, hb * DH), lambda b, h, qi: (rq(b, qi), h)),
        pl.BlockSpec((qb, hb * DH), lambda b, h, qi: (rq(b, qi), nh + h)),
        pl.BlockSpec((L, hb * DH), lambda b, h, qi: (rs(b), 2 * nh + h)),
        pl.BlockSpec((L, hb * DH), lambda b, h, qi: (rs(b), 3 * nh + h)),
        pl.BlockSpec((L, hb * 2 * DH), lambda b, h, qi: (rs(b), 2 * nh + h)),
    ]
    args = [qkv] * 5
    if latent:
        ck, cv = cache
        in_specs += [
            pl.BlockSpec((1, PAST, hb * DH), lambda b, h, qi: (b, 0, h)),
            pl.BlockSpec((1, PAST, hb * DH), lambda b, h, qi: (b, 0, nh + h)),
            pl.BlockSpec((1, PAST, hb * 2 * DH), lambda b, h, qi: (b, 0, h)),
        ]
        args += [ck, ck, cv]
        in_specs += [pl.BlockSpec((qb, DH), lambda b, h, qi: (qi, 0))] * 3
        in_specs += [pl.BlockSpec((L, DH), lambda b, h, qi: (0, 0))] * 3
        args += list(tables) * 2
    in_specs += [pl.BlockSpec((4, DH), lambda b, h, qi: (0, 0)),
                 pl.BlockSpec((1, 2 * DH), lambda b, h, qi: (0, 0))]
    args += [lam_p, subln_g.reshape(1, 2 * DH)]
    body = functools.partial(_diff_kernel, L=L, latent=latent, lambda_init=lambda_init, hb=hb)
    aliases = {}
    if o_prev is not None:
        in_specs.append(pl.BlockSpec(memory_space=pl.ANY))
        args.append(o_prev)
        aliases = {len(args) - 1: 0}
        body = _drop_ref(body, len(args) - 1)
    out_specs = [pl.BlockSpec((qb, hb * 2 * DH), lambda b, h, qi: (rq(b, qi), h))]
    out_shape = [jax.ShapeDtypeStruct((NT, D), BF16)]
    if not latent:
        assert hb == DIFF_H and nq == 1 and row_blk0 == 0
        out_specs += [pl.BlockSpec((L, D), lambda b, h, qi: (b, 0))] * 2
        out_shape += [jax.ShapeDtypeStruct((nb * L, D), F32)] * 2
    return pl.pallas_call(
        body,
        grid=(nb, nh, nq),
        in_specs=in_specs,
        out_specs=out_specs,
        out_shape=out_shape,
        scratch_shapes=[pltpu.VMEM((2, hb, kt, DH), BF16), pltpu.VMEM((hb, kt, 2 * DH), BF16)],
        input_output_aliases=aliases,
        compiler_params=_cparams(("parallel", "parallel", "arbitrary")),
        name="diff_attn",
    )(*args)


def _win_kernel(*refs, L, latent, gb):
    q_ref, k_ref, v_ref = refs[:3]
    pos = 3
    if latent:
        ck_ref, cv_ref, cq_ref, snq_ref, spq_ref, ckk_ref, snk_ref, spk_ref = refs[pos:pos + 8]
        pos += 8
    sink_ref, o_ref = refs[pos:pos + 2]
    pos += 2
    if not latent:
        kout_ref, vout_ref = refs[pos:pos + 2]
        pos += 2
        kout_ref[...] = k_ref[...]
        vout_ref[...] = v_ref[...]
    g = pl.program_id(1)
    n = pl.program_id(2)
    scale = DH ** -0.5
    qpk = WIN_H // WIN_KV

    if latent:
        kbuf, vbuf = refs[pos:pos + 2]
        koff = PAST + WINDOW

        @pl.when(n == 0)
        def _():
            zpad = jnp.zeros((WINDOW, DH), BF16)
            for gg in range(gb):
                cols = slice(gg * DH, (gg + 1) * DH)
                for buf, c_ref, x_ref, rope in ((kbuf, ck_ref, k_ref, True), (vbuf, cv_ref, v_ref, False)):
                    buf[gg, 0:PAST, :] = c_ref[0, :, cols].astype(BF16)
                    buf[gg, PAST:koff, :] = zpad
                    x = x_ref[:, cols]
                    if rope:
                        x = _rope(x, ckk_ref[...], snk_ref[...], spk_ref[...])
                    buf[gg, koff:koff + L, :] = x.astype(BF16)
                    buf[gg, koff + L:koff + L + WINDOW, :] = zpad

        band0 = pl.multiple_of(PAST + n * WINDOW, WINDOW)
        qi_ = lax.broadcasted_iota(jnp.int32, (WINDOW, 3 * WINDOW), 0)
        kj_ = lax.broadcasted_iota(jnp.int32, (WINDOW, 3 * WINDOW), 1)
        kpos = (n - 1) * WINDOW + kj_
        delta = kj_ - qi_
        valid = (delta >= 0) & (delta <= 2 * WINDOW) & (kpos >= 0) & (kpos < L)

    for gg in range(gb):
        if latent:
            kc, vc = kbuf[gg, 0:PAST, :], vbuf[gg, 0:PAST, :]
            kb, vb = kbuf[gg, pl.ds(band0, 3 * WINDOW), :], vbuf[gg, pl.ds(band0, 3 * WINDOW), :]
        else:
            kc = k_ref[:, gg * DH:(gg + 1) * DH].astype(BF16)
            vc = v_ref[:, gg * DH:(gg + 1) * DH].astype(BF16)
        for r in range(qpk):
            cols = slice((gg * qpk + r) * DH, (gg * qpk + r + 1) * DH)
            q = q_ref[:, cols]
            if latent:
                q = _rope(q, cq_ref[...], snq_ref[...], spq_ref[...])
            qb_ = q.astype(BF16)
            sink = sink_ref[(g * gb + gg) * qpk + r]
            s_c = _dot_nt(qb_, kc) * scale
            mx = jnp.maximum(jnp.max(s_c, axis=-1, keepdims=True), sink)
            if latent:
                s_b = jnp.where(valid, _dot_nt(qb_, kb) * scale, -jnp.inf)
                mx = jnp.maximum(mx, jnp.max(s_b, axis=-1, keepdims=True))
            e_c = jnp.exp(s_c - mx)
            den = jnp.sum(e_c, axis=-1, keepdims=True) + jnp.exp(sink - mx)
            if latent:
                e_b = jnp.exp(s_b - mx)
                den = den + jnp.sum(e_b, axis=-1, keepdims=True)
            o = _dot((e_c / den).astype(BF16), vc)
            if latent:
                o = o + _dot((e_b / den).astype(BF16), vb)
            o_ref[:, cols] = o.astype(BF16)


def _win_call(qkv, sink, *, L, nb, row_blk0, qb, gb, cache=None, tables=None, o_prev=None):
    latent = cache is not None
    nq = L // qb
    ng = WIN_KV // gb
    qw = gb * (WIN_H // WIN_KV) * DH
    koff = WIN_H // gb
    voff = (WIN_H + WIN_KV) // gb
    rq = lambda b, qi: (row_blk0 + b) * nq + qi
    rs = lambda b: row_blk0 + b
    in_specs = [
        pl.BlockSpec((qb, qw), lambda b, g, qi: (rq(b, qi), g)),
        pl.BlockSpec((L, gb * DH), lambda b, g, qi: (rs(b), koff + g)),
        pl.BlockSpec((L, gb * DH), lambda b, g, qi: (rs(b), voff + g)),
    ]
    args = [qkv] * 3
    scratch = []
    if latent:
        ck, cv = cache
        in_specs += [pl.BlockSpec((1, PAST, gb * DH), lambda b, g, qi: (b, 0, g))] * 2
        args += [ck, cv]
        in_specs += [pl.BlockSpec((qb, DH), lambda b, g, qi: (qi, 0))] * 3
        in_specs += [pl.BlockSpec((L, DH), lambda b, g, qi: (0, 0))] * 3
        args += list(tables) * 2
        kt = PAST + L + 2 * WINDOW
        scratch = [pltpu.VMEM((gb, kt, DH), BF16), pltpu.VMEM((gb, kt, DH), BF16)]
    in_specs.append(pl.BlockSpec(memory_space=pltpu.SMEM))
    args.append(sink)
    body = functools.partial(_win_kernel, L=L, latent=latent, gb=gb)
    aliases = {}
    if o_prev is not None:
        in_specs.append(pl.BlockSpec(memory_space=pl.ANY))
        args.append(o_prev)
        aliases = {len(args) - 1: 0}
        body = _drop_ref(body, len(args) - 1)
    out_specs = [pl.BlockSpec((qb, qw), lambda b, g, qi: (rq(b, qi), g))]
    out_shape = [jax.ShapeDtypeStruct((NT, D), BF16)]
    if not latent:
        assert gb == WIN_KV and nq == 1 and row_blk0 == 0
        out_specs += [pl.BlockSpec((L, gb * DH), lambda b, g, qi: (b, 0))] * 2
        out_shape += [jax.ShapeDtypeStruct((nb * L, gb * DH), F32)] * 2
    return pl.pallas_call(
        body,
        grid=(nb, ng, nq),
        in_specs=in_specs,
        out_specs=out_specs,
        out_shape=out_shape,
        scratch_shapes=scratch,
        input_output_aliases=aliases,
        compiler_params=_cparams(("parallel", "parallel", "arbitrary")),
        name="win_attn",
    )(*args)


def kernel(x_prompt, x_sample, state_l0_fwd, state_l0_bwd, cache_l1_k, cache_l1_v, cache_l2_k, cache_l2_v, state_l3_fwd, state_l3_bwd, c, c_ctx, norm_g, w_ada, b_ada, ffn1_w_in, ffn1_w_out, ffn2_w_in, ffn2_w_out, ssm_w_in, ssm_conv_w, ssm_conv_b, ssm_dt_bias, ssm_a_log, ssm_d, ssm_norm_g, ssm_w_out, diff_w_qkv, diff_lambda, diff_subln_g, diff_w_out, win_w_qkv, win_sink, win_w_out, final_norm_g):
    x = _stack_tokens(x_prompt.reshape(NP, D), x_sample.reshape(NS, D))
    cv8 = jnp.concatenate([c_ctx[None, :], c, jnp.zeros((8 - 1 - NB_S, D), F32)], axis=0)
    mods = _modulation(cv8, w_ada, b_ada)
    norm_g4 = norm_g.reshape(DEPTH, 3, 1, D)
    tables = _rope_tables()
    ssm_states = [(state_l0_fwd, state_l0_bwd), (state_l3_fwd, state_l3_bwd)]
    new_state = []
    for l in range(DEPTH):
        m, j = l % 3, l // 3
        x = _ffn(x, mods, norm_g4, ffn1_w_in, ffn1_w_out, l, 0)
        if m == 0:
            zx, dt_raw = _inproj(x, mods, norm_g4, ssm_w_in, l, j, 512, n_main=D_INNER + CONV_CH)
            y, hf, hb = _ssm_mixer(zx, dt_raw, *ssm_states[j], ssm_conv_w[j], ssm_conv_b[j],
                                   ssm_dt_bias[j], ssm_a_log[j], ssm_d[j])
            x = _outproj(y, ssm_w_out, j, x, mods, l, norm_g=ssm_norm_g[j])
            new_state += [hf, hb]
        elif m == 1:
            lambda_init = 0.8 - 0.6 * math.exp(-0.3 * l)
            (qkv,) = _inproj(x, mods, norm_g4, diff_w_qkv, l, j, 512)
            o, k_new, v_new = _diff_call(qkv, diff_lambda[j], diff_subln_g[j], lambda_init,
                                         L=L_P, nb=NB_P, row_blk0=0, qb=L_P, hb=DIFF_H)
            cache = (cache_l1_k.reshape(NB_S, PAST, D), cache_l1_v.reshape(NB_S, PAST, D))
            (o,) = _diff_call(qkv, diff_lambda[j], diff_subln_g[j], lambda_init,
                              L=L_S, nb=NB_S, row_blk0=NP // L_S, qb=256, hb=4, cache=cache,
                              tables=tables, o_prev=o)
            x = _outproj(o, diff_w_out, j, x, mods, l)
            new_state += [k_new.reshape(NB_P, L_P, 2, DIFF_H, DH),
                          v_new.reshape(NB_P, L_P, DIFF_H, 2 * DH)]
        else:
            (qkv,) = _inproj(x, mods, norm_g4, win_w_qkv, l, j, 512)
            o, k_new, v_new = _win_call(qkv, win_sink[j], L=L_P, nb=NB_P, row_blk0=0, qb=L_P,
                                        gb=WIN_KV)
            kvw = WIN_KV * DH
            cache = (cache_l2_k.reshape(NB_S, PAST, kvw), cache_l2_v.reshape(NB_S, PAST, kvw))
            (o,) = _win_call(qkv, win_sink[j], L=L_S, nb=NB_S, row_blk0=NP // L_S, qb=WINDOW, gb=4,
                             cache=cache, tables=tables, o_prev=o)
            x = _outproj(o, win_w_out, j, x, mods, l)
            new_state += [k_new.reshape(NB_P, L_P, WIN_KV, DH), v_new.reshape(NB_P, L_P, WIN_KV, DH)]
        x = _ffn(x, mods, norm_g4, ffn2_w_in, ffn2_w_out, l, 2)
    y_p = _final_norm(x, final_norm_g, 0, NP)
    y_s = _final_norm(x, final_norm_g, NP, NS)
    s0f, s0b, k1, v1, k2, v2, s3f, s3b = new_state
    return (y_p.reshape(NB_P, L_P, D), y_s.reshape(NB_S, L_S, D),
            s0f, s0b, k1, v1, k2, v2, s3f, s3b)
```

```python
import functools
import math

import jax
import jax.numpy as jnp
from jax import lax
from jax.experimental import pallas as pl
from jax.experimental.pallas import tpu as pltpu

F32 = jnp.float32
BF16 = jnp.bfloat16

D = 2048
NB_P, L_P = 16, 256
NB_S, L_S = 2, 1024
PAST = 256
NP = NB_P * L_P
NS = NB_S * L_S
NT = NP + NS
DEPTH = 4
N_MOD = 9
D_FF = 5632
EPS = 1e-6
GRID_W = 64
ROPE_BASE = 10000.0

D_INNER = 4096
SSM_HEADS = 64
SSM_P = 64
SSM_G = 8
SSM_N = 128
CONV_CH = D_INNER + 2 * SSM_G * SSM_N
SSM_IN = D_INNER + CONV_CH + 2 * SSM_HEADS
CHUNK = 128
GW = D_INNER // SSM_G

DIFF_H = 8
DH = 128
WIN_H = 16
WIN_KV = 4
WINDOW = 128

TM = 1024
TM_WIDE = 2 * TM
ROW_CHUNK = 256
STACK_ROWS = 512
VMEM_LIMIT = 60 * 1024 * 1024


def _resident_spec(block_shape, index_map):
    return pl.BlockSpec(block_shape, index_map, pipeline_mode=pl.Buffered(1))


def _cparams(sem):
    return pltpu.CompilerParams(dimension_semantics=sem, vmem_limit_bytes=VMEM_LIMIT)


def _drop_ref(kernel_fn, idx):
    def body(*refs):
        return kernel_fn(*(refs[:idx] + refs[idx + 1:]))
    return body


def _silu(x):
    return x * (0.5 * jnp.tanh(0.5 * x) + 0.5)


def _dot(a, b):
    return jnp.dot(a, b, preferred_element_type=F32)


def _dot_nt(a, b):
    return lax.dot_general(a, b, (((1,), (1,)), ((), ())), preferred_element_type=F32)


def _dot_tn(a, b):
    return lax.dot_general(a, b, (((0,), (0,)), ((), ())), preferred_element_type=F32)


def _seg_rows(start):
    return jnp.where(start < NP, 0, 1 + (start - NP) // L_S)


def _seg(i, tm):
    return _seg_rows(i * tm)


def _modnorm(x, g, shift, scale):
    y = x * lax.rsqrt(jnp.mean(x * x, axis=-1, keepdims=True) + EPS) * g
    return y * (1.0 + scale) + shift


def _stack_kernel(xp_ref, xs_ref, o_ref):
    i = pl.program_id(0)

    @pl.when(i < NP // STACK_ROWS)
    def _():
        o_ref[...] = xp_ref[...]

    @pl.when(i >= NP // STACK_ROWS)
    def _():
        o_ref[...] = xs_ref[...]


def _stack_tokens(xp, xs):
    npb = NP // STACK_ROWS
    return pl.pallas_call(
        _stack_kernel,
        grid=(NT // STACK_ROWS,),
        in_specs=[pl.BlockSpec((STACK_ROWS, D), lambda i: (jnp.minimum(i, npb - 1), 0)),
                  pl.BlockSpec((STACK_ROWS, D), lambda i: (jnp.maximum(i - npb, 0), 0))],
        out_specs=pl.BlockSpec((STACK_ROWS, D), lambda i: (i, 0)),
        out_shape=jax.ShapeDtypeStruct((NT, D), F32),
        compiler_params=_cparams(("arbitrary",)),
        name="stack_tokens",
    )(xp, xs)


def _mod_kernel(cv_ref, w_ref, b_ref, o_ref):
    @pl.when(pl.program_id(2) == 0)
    def _():
        o_ref[0, 0] = jnp.broadcast_to(b_ref[0, 0], (8, D))

    cv = cv_ref[...]
    o_ref[0, 0] += _dot(_silu(cv).astype(BF16), w_ref[0].astype(BF16))


def _modulation(cv8, w_ada, b_ada):
    kc = 1024
    out = pl.pallas_call(
        _mod_kernel,
        grid=(DEPTH, N_MOD, D // kc),
        in_specs=[
            pl.BlockSpec((8, kc), lambda l, j, k: (0, k)),
            pl.BlockSpec((1, kc, D), lambda l, j, k: (l, k, j)),
            pl.BlockSpec((1, 1, 1, D), lambda l, j, k: (l, j, 0, 0)),
        ],
        out_specs=pl.BlockSpec((1, 1, 8, D), lambda l, j, k: (l, j, 0, 0)),
        out_shape=jax.ShapeDtypeStruct((DEPTH, N_MOD, 8, D), F32),
        compiler_params=_cparams(("parallel", "parallel", "arbitrary")),
        name="modulation",
    )(cv8, w_ada, b_ada.reshape(DEPTH, N_MOD, 1, D))
    return out.reshape(DEPTH, N_MOD, 8, 1, D)


def _mod_spec(l, k):
    return pl.BlockSpec((1, 1, 1, 1, D), lambda i, j: (l, k, _seg(i, TM), 0, 0))


def _mod_spec_wide(l, k, half):
    return pl.BlockSpec((1, 1, 1, 1, D), lambda i, j: (l, k, _seg_rows(i * TM_WIDE + half * TM), 0, 0))


def _ffn_kernel(x_ref, sh_ref, sc_ref, gt_ref, g_ref, wg_ref, wu_ref, wo_ref, o_ref, h_scr):
    j = pl.program_id(1)

    def weights():
        return wg_ref[0].astype(BF16), wu_ref[0].astype(BF16), wo_ref[0].astype(BF16)

    def hidden_chunk(h, wg, wu, wo):
        act = (_silu(_dot(h, wg)) * _dot(h, wu)).astype(BF16)
        return _dot(act, wo)

    @pl.when(j == 0)
    def _():
        w = weights()
        for r in range(TM // ROW_CHUNK):
            rows = pl.ds(r * ROW_CHUNK, ROW_CHUNK)
            h = _modnorm(x_ref[rows, :], g_ref[0, 0], sh_ref[0, 0, 0], sc_ref[0, 0, 0]).astype(BF16)
            h_scr[rows, :] = h
            o_ref[rows, :] = hidden_chunk(h, *w)

    last = pl.num_programs(1) - 1

    @pl.when(jnp.logical_and(j > 0, j < last))
    def _():
        o_ref[...] += hidden_chunk(h_scr[...], *weights())

    @pl.when(j == last)
    def _():
        ffn = o_ref[...] + hidden_chunk(h_scr[...], *weights())
        o_ref[...] = x_ref[...] + 0.5 * gt_ref[0, 0, 0] * ffn


def _ffn(x, mods, norm_g4, w_in, w_out, l, sub):
    tf = 256
    nf = D_FF // tf
    return pl.pallas_call(
        _ffn_kernel,
        grid=(NT // TM, nf),
        in_specs=[
            pl.BlockSpec((TM, D), lambda i, j: (i, 0)),
            _mod_spec(l, 3 * sub), _mod_spec(l, 3 * sub + 1), _mod_spec(l, 3 * sub + 2),
            pl.BlockSpec((1, 1, 1, D), lambda i, j: (l, sub, 0, 0)),
            pl.BlockSpec((1, D, tf), lambda i, j: (l, 0, j)),
            pl.BlockSpec((1, D, tf), lambda i, j: (l, 0, j + nf)),
            pl.BlockSpec((1, tf, D), lambda i, j: (l, j, 0)),
        ],
        out_specs=pl.BlockSpec((TM, D), lambda i, j: (i, 0)),
        out_shape=jax.ShapeDtypeStruct((NT, D), F32),
        scratch_shapes=[pltpu.VMEM((TM, D), BF16)],
        compiler_params=_cparams(("parallel", "arbitrary")),
        name="ffn",
    )(x, mods, mods, mods, norm_g4, w_in, w_in, w_out)


def _inproj_kernel(*refs, tail):
    if tail:
        x_ref, sh0_ref, sh1_ref, sc0_ref, sc1_ref, g_ref, w_ref, wt_ref, o_ref, ot_ref, h_scr = refs
    else:
        x_ref, sh0_ref, sh1_ref, sc0_ref, sc1_ref, g_ref, w_ref, o_ref, h_scr = refs
    j = pl.program_id(1)

    @pl.when(j == 0)
    def _():
        w = w_ref[0].astype(BF16)
        if tail:
            wt = wt_ref[0].astype(BF16)
        for r in range(TM_WIDE // ROW_CHUNK):
            rows = pl.ds(r * ROW_CHUNK, ROW_CHUNK)
            sh_ref, sc_ref = ((sh0_ref, sc0_ref), (sh1_ref, sc1_ref))[r * ROW_CHUNK // TM]
            h = _modnorm(x_ref[rows, :], g_ref[0, 0], sh_ref[0, 0, 0], sc_ref[0, 0, 0]).astype(BF16)
            h_scr[rows, :] = h
            o_ref[rows, :] = _dot(h, w)
            if tail:
                ot_ref[rows, :] = _dot(h, wt)

    @pl.when(j > 0)
    def _():
        o_ref[...] = _dot(h_scr[...], w_ref[0].astype(BF16))


def _inproj(x, mods, norm_g4, w, l, j_w, tn, n_main=None):
    n = w.shape[-1]
    n_main = n if n_main is None else n_main
    n_tail = n - n_main
    in_specs = [
        _resident_spec((TM_WIDE, D), lambda i, j: (i, 0)),
        _mod_spec_wide(l, 3, 0), _mod_spec_wide(l, 3, 1),
        _mod_spec_wide(l, 4, 0), _mod_spec_wide(l, 4, 1),
        pl.BlockSpec((1, 1, 1, D), lambda i, j: (l, 1, 0, 0)),
        pl.BlockSpec((1, D, tn), lambda i, j: (j_w, 0, j)),
    ]
    args = [x, mods, mods, mods, mods, norm_g4, w]
    out_specs = [pl.BlockSpec((TM_WIDE, tn), lambda i, j: (i, j))]
    out_shape = [jax.ShapeDtypeStruct((NT, n_main), F32)]
    if n_tail:
        in_specs.append(pl.BlockSpec((1, D, n_tail), lambda i, j: (j_w, 0, n_main // n_tail)))
        args.append(w)
        out_specs.append(pl.BlockSpec((TM_WIDE, n_tail), lambda i, j: (i, 0)))
        out_shape.append(jax.ShapeDtypeStruct((NT, n_tail), F32))
    return pl.pallas_call(
        functools.partial(_inproj_kernel, tail=bool(n_tail)),
        grid=(NT // TM_WIDE, n_main // tn),
        in_specs=in_specs,
        out_specs=out_specs,
        out_shape=out_shape,
        scratch_shapes=[pltpu.VMEM((TM_WIDE, D), BF16)],
        compiler_params=_cparams(("parallel", "arbitrary")),
        name="inproj",
    )(*args)


def _outproj_kernel(*refs, norm):
    if norm:
        y_ref, ng_ref, w_ref, x_ref, gt_ref, o_ref, yn_scr = refs
        j = pl.program_id(1)

        @pl.when(j == 0)
        def _():
            w = w_ref[0].astype(BF16)
            for r in range(TM // ROW_CHUNK):
                rows = pl.ds(r * ROW_CHUNK, ROW_CHUNK)
                y = y_ref[rows, :].astype(F32)
                yn = y * lax.rsqrt(jnp.mean(y * y, axis=-1, keepdims=True) + EPS) * ng_ref[...]
                yn = yn.astype(BF16)
                yn_scr[rows, :] = yn
                o_ref[rows, :] = x_ref[rows, :] + gt_ref[0, 0, 0] * _dot(yn, w)

        @pl.when(j > 0)
        def _():
            o_ref[...] = x_ref[...] + gt_ref[0, 0, 0] * _dot(yn_scr[...], w_ref[0].astype(BF16))
    else:
        y_ref, w_ref, x_ref, gt0_ref, gt1_ref, o_ref = refs
        w = w_ref[0].astype(BF16)
        for half, gt_ref in enumerate((gt0_ref, gt1_ref)):
            rows = pl.ds(half * TM, TM)
            o_ref[rows, :] = x_ref[rows, :] + gt_ref[0, 0, 0] * _dot(y_ref[rows, :], w)


def _outproj(y, w, j_w, x, mods, l, norm_g=None):
    tn = 512
    k = y.shape[-1]
    norm = norm_g is not None
    tm = TM if norm else TM_WIDE
    in_specs = [pl.BlockSpec((tm, k), lambda i, j: (i, 0))]
    args = [y]
    scratch = []
    if norm:
        in_specs.append(pl.BlockSpec((1, k), lambda i, j: (0, 0)))
        args.append(norm_g.reshape(1, k))
        scratch.append(pltpu.VMEM((tm, k), BF16))
    in_specs += [
        pl.BlockSpec((1, k, tn), lambda i, j: (j_w, 0, j)),
        pl.BlockSpec((tm, tn), lambda i, j: (i, j)),
    ]
    args += [w, x]
    for half in range(tm // TM):
        in_specs.append(pl.BlockSpec((1, 1, 1, 1, tn),
                                     lambda i, j, half=half: (l, 5, _seg_rows(i * tm + half * TM), 0, j)))
        args.append(mods)
    if not norm:
        in_specs[1] = pl.BlockSpec((1, k, tn), lambda i, j: (j_w, 0, j), pipeline_mode=pl.Buffered(3))
        out_spec = pl.BlockSpec((tm, tn), lambda i, j: (i, j))

        def streamed(*refs):
            pltpu.emit_pipeline(functools.partial(_outproj_kernel, norm=False),
                                grid=(NT // tm, D // tn), in_specs=in_specs,
                                out_specs=[out_spec])(*refs)

        return pl.pallas_call(
            streamed,
            in_specs=[pl.BlockSpec(memory_space=pl.ANY)] * len(args),
            out_specs=pl.BlockSpec(memory_space=pl.ANY),
            out_shape=jax.ShapeDtypeStruct((NT, D), F32),
            compiler_params=pltpu.CompilerParams(vmem_limit_bytes=VMEM_LIMIT),
            name="outproj",
        )(*args)
    return pl.pallas_call(
        functools.partial(_outproj_kernel, norm=norm),
        grid=(NT // tm, D // tn),
        in_specs=in_specs,
        out_specs=pl.BlockSpec((tm, tn), lambda i, j: (i, j)),
        out_shape=jax.ShapeDtypeStruct((NT, D), F32),
        scratch_shapes=scratch,
        compiler_params=_cparams(("parallel", "arbitrary")),
        name="outproj",
    )(*args)


def _rmsnorm_kernel(x_ref, g_ref, o_ref):
    x = x_ref[...]
    o_ref[...] = x * lax.rsqrt(jnp.mean(x * x, axis=-1, keepdims=True) + EPS) * g_ref[...]


def _final_norm(x, g, row0, nrows):
    blk0 = row0 // TM
    return pl.pallas_call(
        _rmsnorm_kernel,
        grid=(nrows // TM,),
        in_specs=[pl.BlockSpec((TM, D), lambda i: (blk0 + i, 0)),
                  pl.BlockSpec((1, D), lambda i: (0, 0))],
        out_specs=pl.BlockSpec((TM, D), lambda i: (i, 0)),
        out_shape=jax.ShapeDtypeStruct((nrows, D), F32),
        compiler_params=_cparams(("parallel",)),
        name="final_norm",
    )(x, g.reshape(1, D))


def _split3(x):
    hi = x.astype(BF16)
    r = x - hi.astype(F32)
    mid = r.astype(BF16)
    lo = (r - mid.astype(F32)).astype(BF16)
    return hi, mid, lo


def _softplus(x):
    return jnp.maximum(x, 0.0) + jnp.log1p(jnp.exp(-jnp.abs(x)))


def _ssd_kernel(*refs, L, has_h0, emit_state):
    (z_ref, xs_ref, bm_ref, cm_ref, dt_ref, dtr_ref, cwx_ref, cwb_ref, cwc_ref,
     cbx_ref, cbb_ref, cbc_ref, dtb_ref, dtbr_ref, alr_ref, ala_ref, e_ref, dsk_ref) = refs[:18]
    pos = 18
    if has_h0:
        h0f_ref, h0b_ref = refs[pos:pos + 2]
        pos += 2
    y_ref = refs[pos]
    pos += 1
    if emit_state:
        hf_ref, hb_ref = refs[pos:pos + 2]
        pos += 2
    xs_s, bm_s, cm_s, dt_s, yf_s, yb_s, ht_s, dte_s, ace_s, acr_s = refs[pos:pos + 10]
    nc = L // CHUNK
    rowc = lax.broadcasted_iota(jnp.int32, (CHUNK, 1), 0)
    a_all = -jnp.exp(ala_ref[...])
    a_r = -jnp.exp(alr_ref[0])
    dtb_r = dtbr_ref[0]
    ii = lax.broadcasted_iota(jnp.int32, (CHUNK, CHUNK), 0)
    jj = lax.broadcasted_iota(jnp.int32, (CHUNK, CHUNK), 1)
    lower = ii >= jj
    upper = ii <= jj
    lane_lo = jj < SSM_P
    lower_k = jnp.concatenate([lower.astype(BF16)] * 3, axis=1)
    upper_k = jnp.concatenate([upper.astype(BF16)] * 3, axis=1)
    lower_r = jnp.concatenate([lower.astype(BF16)] * 3, axis=0)
    upper_r = jnp.concatenate([upper.astype(BF16)] * 3, axis=0)

    def prep(t, carry):
        r0 = pl.multiple_of(t * CHUNK, CHUNK)
        rows = pl.ds(r0, CHUNK)
        before = pl.ds(pl.multiple_of(jnp.maximum(r0 - 8, 0), 8), 8)
        after = pl.ds(pl.multiple_of(jnp.minimum(r0 + CHUNK, L - 8), 8), 8)
        for x_ref, w_ref, b_ref, out_s in ((xs_ref, cwx_ref, cbx_ref, xs_s),
                                           (bm_ref, cwb_ref, cbb_ref, bm_s),
                                           (cm_ref, cwc_ref, cbc_ref, cm_s)):
            x = x_ref[rows, :]
            prev = jnp.where(t == 0, 0.0, x_ref[before, :][7:8])
            nxt = jnp.where(t == nc - 1, 0.0, x_ref[after, :][0:1])
            xm = jnp.where(rowc == 0, prev, pltpu.roll(x, 1, 0))
            xp = jnp.where(rowc == CHUNK - 1, nxt, pltpu.roll(x, CHUNK - 1, 0))
            w = w_ref[...]
            out_s[rows, :] = _silu(w[0:1] * xm + w[1:2] * x + w[2:3] * xp + b_ref[...])
        dt_s[rows, :] = _softplus(dt_ref[rows, :] + dtb_ref[...])
        return carry

    lax.fori_loop(0, nc, prep, 0)

    def decays(t, carry):
        rows = pl.ds(pl.multiple_of(t * CHUNK, CHUNK), CHUNK)
        dt_c = dt_s[rows, :]
        dta_c = jnp.concatenate(_split3(dt_c * a_all), axis=0)
        dt_hm = jnp.concatenate(_split3(dt_c)[:2], axis=1)
        dta_r = _softplus(dtr_ref[0, t] + dtb_r) * a_r
        dta_r = jnp.concatenate(_split3(dta_r), axis=1)
        for d in range(2):
            m_col, m_row = (lower_k, upper_r) if d == 0 else (upper_k, lower_r)
            acum_c = _dot(m_col, dta_c)
            acr_s[d, t] = _dot(dta_r, m_row)
            e_d = e_ref[0, d]
            dte_s[d, rows, :] = _dot(dt_hm, e_d)
            ace_s[d, rows, :] = _dot(jnp.concatenate(_split3(acum_c)[:2], axis=1), e_d)
        return carry

    lax.fori_loop(0, nc, decays, 0, unroll=2)

    if has_h0:
        ht_s[0] = jnp.transpose(h0f_ref[0])
        ht_s[1] = jnp.transpose(h0b_ref[0])
    else:
        ht_s[...] = jnp.zeros_like(ht_s)

    def direction(d, c):
        cc = c if d == 0 else nc - 1 - c
        r0 = pl.multiple_of(cc * CHUNK, CHUNK)
        xs_c = xs_s[pl.ds(r0, CHUNK), :]
        bm_b = bm_s[pl.ds(r0, CHUNK), :].astype(BF16)
        cm_b = cm_s[pl.ds(r0, CHUNK), :].astype(BF16)
        mask = lower if d == 0 else upper
        dt_e = dte_s[d, pl.ds(r0, CHUNK), :]
        acum_e = ace_s[d, pl.ds(r0, CHUNK), :]
        acum_r = acr_s[d, cc]
        last_e = acum_e[CHUNK - 1:CHUNK, :] if d == 0 else acum_e[0:1, :]
        cb = _dot_nt(cm_b, bm_b)
        xdt = xs_c * dt_e
        xdtd = (xdt * jnp.exp(last_e - acum_e)).astype(BF16)
        ht = ht_s[d]
        y_off = _dot(cm_b, ht.astype(BF16)) * jnp.exp(acum_e)
        ht_s[d] = ht * jnp.exp(last_e) + _dot_tn(bm_b, xdtd)
        for pair in range(4):
            ra = d * 8 + 2 * pair
            lanes = slice(pair * CHUNK, (pair + 1) * CHUNK)
            blk = acum_e[:, lanes]
            swp = pltpu.roll(blk, SSM_P, 1)
            ms = []
            for r, col in ((ra, jnp.where(lane_lo, blk, swp)), (ra + 1, jnp.where(lane_lo, swp, blk))):
                seg = col - acum_r[r:r + 1, :]
                decay = jnp.exp(jnp.where(mask, seg, -jnp.inf))
                ms.append((cb * decay).astype(BF16))
            m2 = jnp.concatenate(ms, axis=1)
            xb = xdt[:, lanes].astype(BF16)
            zero = jnp.zeros_like(xb)
            xbd = jnp.concatenate([jnp.where(lane_lo, xb, zero),
                                   jnp.where(lane_lo, zero, xb)], axis=0)
            out = yf_s if d == 0 else yb_s
            out[pl.ds(r0, CHUNK), lanes] = _dot(m2, xbd) + y_off[:, lanes]

    def chunk_body(c, carry):
        direction(0, c)
        direction(1, c)
        return carry

    lax.fori_loop(0, nc, chunk_body, 0, unroll=2)

    def finish(t, carry):
        rows = pl.ds(pl.multiple_of(t * CHUNK, CHUNK), CHUNK)
        y = yf_s[rows, :] + yb_s[rows, :] + dsk_ref[...] * xs_s[rows, :]
        y_ref[rows, :] = (y * _silu(z_ref[rows, :])).astype(y_ref.dtype)
        return carry

    lax.fori_loop(0, nc, finish, 0)

    if emit_state:
        hf_ref[0] = jnp.transpose(ht_s[0])
        hb_ref[0] = jnp.transpose(ht_s[1])


def _ssd_call(zx, dt_raw, dtr, conv_w, conv_b, dtb, dtb_r, al_r, al_all, e_sel, dsk, *,
              L, nb, row_blk0, h0=None, emit_state=False, y_prev=None):
    nc = L // CHUNK
    has_h0 = h0 is not None
    rb = lambda b: row_blk0 + b
    xoff = D_INNER // GW
    boff = 2 * D_INNER // SSM_N
    coff = boff + SSM_G
    in_specs = [
        pl.BlockSpec((L, GW), lambda b, g: (rb(b), g)),
        pl.BlockSpec((L, GW), lambda b, g: (rb(b), xoff + g)),
        pl.BlockSpec((L, SSM_N), lambda b, g: (rb(b), boff + g)),
        pl.BlockSpec((L, SSM_N), lambda b, g: (rb(b), coff + g)),
        pl.BlockSpec((L, 128), lambda b, g: (rb(b), 0)),
        pl.BlockSpec((1, nc, 16, CHUNK), lambda b, g: (g, rb(b), 0, 0)),
        pl.BlockSpec((3, GW), lambda b, g: (0, g)),
        pl.BlockSpec((3, SSM_N), lambda b, g: (0, D_INNER // SSM_N + g)),
        pl.BlockSpec((3, SSM_N), lambda b, g: (0, D_INNER // SSM_N + SSM_G + g)),
        pl.BlockSpec((1, GW), lambda b, g: (0, g)),
        pl.BlockSpec((1, SSM_N), lambda b, g: (0, D_INNER // SSM_N + g)),
        pl.BlockSpec((1, SSM_N), lambda b, g: (0, D_INNER // SSM_N + SSM_G + g)),
        pl.BlockSpec((1, 128), lambda b, g: (0, 0)),
        pl.BlockSpec((1, 16, 1), lambda b, g: (g, 0, 0)),
        pl.BlockSpec((1, 16, 1), lambda b, g: (g, 0, 0)),
        pl.BlockSpec((1, 128), lambda b, g: (0, 0)),
        pl.BlockSpec((1, 2, 256, GW), lambda b, g: (g, 0, 0, 0)),
        pl.BlockSpec((1, GW), lambda b, g: (0, g)),
    ]
    args = [zx, zx, zx, zx, dt_raw, dtr, conv_w, conv_w, conv_w, conv_b, conv_b, conv_b,
            dtb, dtb_r, al_r, al_all, e_sel, dsk]
    st_spec = pl.BlockSpec((1, GW, SSM_N), lambda b, g: (b, g, 0))
    if has_h0:
        in_specs += [st_spec, st_spec]
        args += [h0[0], h0[1]]
    out_specs = [pl.BlockSpec((L, GW), lambda b, g: (rb(b), g))]
    out_shape = [jax.ShapeDtypeStruct((NT, D_INNER), BF16)]
    if emit_state:
        out_specs += [st_spec, st_spec]
        out_shape += [jax.ShapeDtypeStruct((nb, D_INNER, SSM_N), F32)] * 2
    body = functools.partial(_ssd_kernel, L=L, has_h0=has_h0, emit_state=emit_state)
    aliases = {}
    if y_prev is not None:
        in_specs.append(pl.BlockSpec(memory_space=pl.ANY))
        args.append(y_prev)
        aliases = {len(args) - 1: 0}
        body = _drop_ref(body, len(args) - 1)
    return pl.pallas_call(
        body,
        grid=(nb, SSM_G),
        in_specs=in_specs,
        out_specs=out_specs,
        out_shape=out_shape,
        scratch_shapes=[pltpu.VMEM((L, GW), F32), pltpu.VMEM((L, SSM_N), F32),
                        pltpu.VMEM((L, SSM_N), F32), pltpu.VMEM((L, 128), F32),
                        pltpu.VMEM((L, GW), F32), pltpu.VMEM((L, GW), F32),
                        pltpu.VMEM((2, SSM_N, GW), F32),
                        pltpu.VMEM((2, L, GW), F32), pltpu.VMEM((2, L, GW), F32),
                        pltpu.VMEM((2, L // CHUNK, 16, CHUNK), F32)],
        input_output_aliases=aliases,
        compiler_params=_cparams(("parallel", "parallel")),
        name="ssd",
    )(*args)


def _ssm_mixer(zx, dt_raw, h0_f, h0_b, conv_w, conv_b, dt_bias, a_log, d_skip):
    dt_g = dt_raw.reshape(NT, 2, SSM_G, 8).transpose(2, 0, 1, 3).reshape(SSM_G, NT, 16)
    dtr = dt_g.reshape(SSM_G, NT // CHUNK, CHUNK, 16).transpose(0, 1, 3, 2)

    def per_group(p):
        return p.reshape(2, SSM_G, 8).transpose(1, 0, 2).reshape(SSM_G, 16, 1)

    src = jnp.arange(2 * SSM_HEADS)[None, None, :, None]
    dst = (jnp.arange(2)[None, :, None, None] * SSM_HEADS + jnp.arange(SSM_G)[:, None, None, None] * 8
           + jnp.arange(GW)[None, None, None, :] // SSM_P)
    e_sel = (src == dst).astype(BF16)
    e_sel = jnp.concatenate([e_sel, e_sel], axis=2)
    dsk = jnp.repeat(d_skip, SSM_P).reshape(1, D_INNER)
    common = (zx, dt_raw, dtr, conv_w, conv_b.reshape(1, CONV_CH), dt_bias.reshape(1, 2 * SSM_HEADS),
              per_group(dt_bias), per_group(a_log), a_log.reshape(1, 2 * SSM_HEADS), e_sel, dsk)
    y, hf, hb = _ssd_call(*common, L=L_P, nb=NB_P, row_blk0=0, emit_state=True)
    h0 = (h0_f.reshape(NB_S, D_INNER, SSM_N), h0_b.reshape(NB_S, D_INNER, SSM_N))
    (y,) = _ssd_call(*common, L=L_S, nb=NB_S, row_blk0=NP // L_S, h0=h0, y_prev=y)
    st = (NB_P, SSM_HEADS, SSM_P, SSM_N)
    return y, hf.reshape(st), hb.reshape(st)


def _rope_tables():
    rows = L_S // GRID_W
    rowp = jnp.repeat(jnp.arange(rows, dtype=F32), GRID_W)
    colp = jnp.tile(jnp.arange(GRID_W, dtype=F32), rows)
    nf = DH // 4
    inv = ROPE_BASE ** (-jnp.arange(nf, dtype=F32) / nf)
    a0, a1 = rowp[:, None] * inv, colp[:, None] * inv
    c0, c1, s0, s1 = jnp.cos(a0), jnp.cos(a1), jnp.sin(a0), jnp.sin(a1)
    zz = jnp.zeros_like(s0)
    cos = jnp.concatenate([c0, c0, c1, c1], axis=-1)
    sin_next = jnp.concatenate([-s0, zz, -s1, zz], axis=-1)
    sin_prev = jnp.concatenate([zz, s0, zz, s1], axis=-1)
    return cos, sin_next, sin_prev


def _rope(x, cos, sin_next, sin_prev):
    return x * cos + pltpu.roll(x, 96, 1) * sin_next + pltpu.roll(x, 32, 1) * sin_prev


def _diff_kernel(*refs, L, latent, lambda_init, hb):
    q0_ref, q1_ref, k0_ref, k1_ref, v_ref = refs[:5]
    pos = 5
    if latent:
        ck0_ref, ck1_ref, cv_ref, cq_ref, snq_ref, spq_ref, ck_ref, snk_ref, spk_ref = refs[pos:pos + 9]
        pos += 9
    lam_ref, g_ref, o_ref = refs[pos:pos + 3]
    pos += 3
    if not latent:
        kout_ref, vout_ref = refs[pos:pos + 2]
        pos += 2
    kbuf, vbuf = refs[pos:pos + 2]
    off = PAST if latent else 0

    @pl.when(pl.program_id(2) == 0)
    def _():
        if not latent:
            kout_ref[:, 0:hb * DH] = k0_ref[...]
            kout_ref[:, hb * DH:2 * hb * DH] = k1_ref[...]
            vout_ref[...] = v_ref[...]
        for hh in range(hb):
            c1 = slice(hh * DH, (hh + 1) * DH)
            c2 = slice(hh * 2 * DH, (hh + 1) * 2 * DH)
            if latent:
                kbuf[0, hh, 0:PAST, :] = ck0_ref[0, :, c1].astype(BF16)
                kbuf[1, hh, 0:PAST, :] = ck1_ref[0, :, c1].astype(BF16)
                vbuf[hh, 0:PAST, :] = cv_ref[0, :, c2].astype(BF16)
            for m, k_ref in enumerate((k0_ref, k1_ref)):
                k = k_ref[:, c1]
                if latent:
                    k = _rope(k, ck_ref[...], snk_ref[...], spk_ref[...])
                kbuf[m, hh, off:off + L, :] = k.astype(BF16)
            vbuf[hh, off:off + L, :] = v_ref[:, c2].astype(BF16)

    lv = lam_ref[...]
    lam = (jnp.exp(jnp.sum(lv[0:1] * lv[1:2], axis=1, keepdims=True))
           - jnp.exp(jnp.sum(lv[2:3] * lv[3:4], axis=1, keepdims=True)) + lambda_init)
    for hh in range(hb):
        c1 = slice(hh * DH, (hh + 1) * DH)
        c2 = slice(hh * 2 * DH, (hh + 1) * 2 * DH)
        ps = []
        for m, q_ref in enumerate((q0_ref, q1_ref)):
            q = q_ref[:, c1]
            if latent:
                q = _rope(q, cq_ref[...], snq_ref[...], spq_ref[...])
            s = _dot_nt(q.astype(BF16), kbuf[m, hh]) * (DH ** -0.5)
            e = jnp.exp(s - jnp.max(s, axis=-1, keepdims=True))
            ps.append(e / jnp.sum(e, axis=-1, keepdims=True))
        p = (ps[0] - lam * ps[1]).astype(BF16)
        o = _dot(p, vbuf[hh])
        o = o * lax.rsqrt(jnp.mean(o * o, axis=-1, keepdims=True) + EPS) * g_ref[...]
        o_ref[:, c2] = (o * (1.0 - lambda_init)).astype(BF16)


def _diff_call(qkv, lam_p, subln_g, lambda_init, *, L, nb, row_blk0, qb, hb, cache=None, tables=None,
               o_prev=None):
    latent = cache is not None
    nq = L // qb
    nh = DIFF_H // hb
    kt = L + (PAST if latent else 0)
    rq = lambda b, qi: (row_blk0 + b) * nq + qi
    rs = lambda b: row_blk0 + b
    in_specs = [
        pl.BlockSpec((qb, hb * DH), lambda b, h, qi: (rq(b, qi), h)),
        pl.BlockSpec((qb, hb * DH), lambda b, h, qi: (rq(b, qi), nh + h)),
        pl.BlockSpec((L, hb * DH), lambda b, h, qi: (rs(b), 2 * nh + h)),
        pl.BlockSpec((L, hb * DH), lambda b, h, qi: (rs(b), 3 * nh + h)),
        pl.BlockSpec((L, hb * 2 * DH), lambda b, h, qi: (rs(b), 2 * nh + h)),
    ]
    args = [qkv] * 5
    if latent:
        ck, cv = cache
        in_specs += [
            pl.BlockSpec((1, PAST, hb * DH), lambda b, h, qi: (b, 0, h)),
            pl.BlockSpec((1, PAST, hb * DH), lambda b, h, qi: (b, 0, nh + h)),
            pl.BlockSpec((1, PAST, hb * 2 * DH), lambda b, h, qi: (b, 0, h)),
        ]
        args += [ck, ck, cv]
        in_specs += [pl.BlockSpec((qb, DH), lambda b, h, qi: (qi, 0))] * 3
        in_specs += [pl.BlockSpec((L, DH), lambda b, h, qi: (0, 0))] * 3
        args += list(tables) * 2
    in_specs += [pl.BlockSpec((4, DH), lambda b, h, qi: (0, 0)),
                 pl.BlockSpec((1, 2 * DH), lambda b, h, qi: (0, 0))]
    args += [lam_p, subln_g.reshape(1, 2 * DH)]
    body = functools.partial(_diff_kernel, L=L, latent=latent, lambda_init=lambda_init, hb=hb)
    aliases = {}
    if o_prev is not None:
        in_specs.append(pl.BlockSpec(memory_space=pl.ANY))
        args.append(o_prev)
        aliases = {len(args) - 1: 0}
        body = _drop_ref(body, len(args) - 1)
    out_specs = [pl.BlockSpec((qb, hb * 2 * DH), lambda b, h, qi: (rq(b, qi), h))]
    out_shape = [jax.ShapeDtypeStruct((NT, D), BF16)]
    if not latent:
        assert hb == DIFF_H and nq == 1 and row_blk0 == 0
        out_specs += [pl.BlockSpec((L, D), lambda b, h, qi: (b, 0))] * 2
        out_shape += [jax.ShapeDtypeStruct((nb * L, D), F32)] * 2
    return pl.pallas_call(
        body,
        grid=(nb, nh, nq),
        in_specs=in_specs,
        out_specs=out_specs,
        out_shape=out_shape,
        scratch_shapes=[pltpu.VMEM((2, hb, kt, DH), BF16), pltpu.VMEM((hb, kt, 2 * DH), BF16)],
        input_output_aliases=aliases,
        compiler_params=_cparams(("parallel", "parallel", "arbitrary")),
        name="diff_attn",
    )(*args)


def _win_kernel(*refs, L, latent, gb):
    q_ref, k_ref, v_ref = refs[:3]
    pos = 3
    if latent:
        ck_ref, cv_ref, cq_ref, snq_ref, spq_ref, ckk_ref, snk_ref, spk_ref = refs[pos:pos + 8]
        pos += 8
    sink_ref, o_ref = refs[pos:pos + 2]
    pos += 2
    if not latent:
        kout_ref, vout_ref = refs[pos:pos + 2]
        pos += 2
        kout_ref[...] = k_ref[...]
        vout_ref[...] = v_ref[...]
    g = pl.program_id(1)
    n = pl.program_id(2)
    scale = DH ** -0.5
    qpk = WIN_H // WIN_KV

    if latent:
        kbuf, vbuf = refs[pos:pos + 2]
        koff = PAST + WINDOW

        @pl.when(n == 0)
        def _():
            zpad = jnp.zeros((WINDOW, DH), BF16)
            for gg in range(gb):
                cols = slice(gg * DH, (gg + 1) * DH)
                for buf, c_ref, x_ref, rope in ((kbuf, ck_ref, k_ref, True), (vbuf, cv_ref, v_ref, False)):
                    buf[gg, 0:PAST, :] = c_ref[0, :, cols].astype(BF16)
                    buf[gg, PAST:koff, :] = zpad
                    x = x_ref[:, cols]
                    if rope:
                        x = _rope(x, ckk_ref[...], snk_ref[...], spk_ref[...])
                    buf[gg, koff:koff + L, :] = x.astype(BF16)
                    buf[gg, koff + L:koff + L + WINDOW, :] = zpad

        band0 = pl.multiple_of(PAST + n * WINDOW, WINDOW)
        qi_ = lax.broadcasted_iota(jnp.int32, (WINDOW, 3 * WINDOW), 0)
        kj_ = lax.broadcasted_iota(jnp.int32, (WINDOW, 3 * WINDOW), 1)
        kpos = (n - 1) * WINDOW + kj_
        delta = kj_ - qi_
        valid = (delta >= 0) & (delta <= 2 * WINDOW) & (kpos >= 0) & (kpos < L)

    for gg in range(gb):
        if latent:
            kc, vc = kbuf[gg, 0:PAST, :], vbuf[gg, 0:PAST, :]
            kb, vb = kbuf[gg, pl.ds(band0, 3 * WINDOW), :], vbuf[gg, pl.ds(band0, 3 * WINDOW), :]
        else:
            kc = k_ref[:, gg * DH:(gg + 1) * DH].astype(BF16)
            vc = v_ref[:, gg * DH:(gg + 1) * DH].astype(BF16)
        for r in range(qpk):
            cols = slice((gg * qpk + r) * DH, (gg * qpk + r + 1) * DH)
            q = q_ref[:, cols]
            if latent:
                q = _rope(q, cq_ref[...], snq_ref[...], spq_ref[...])
            qb_ = q.astype(BF16)
            sink = sink_ref[(g * gb + gg) * qpk + r]
            s_c = _dot_nt(qb_, kc) * scale
            mx = jnp.maximum(jnp.max(s_c, axis=-1, keepdims=True), sink)
            if latent:
                s_b = jnp.where(valid, _dot_nt(qb_, kb) * scale, -jnp.inf)
                mx = jnp.maximum(mx, jnp.max(s_b, axis=-1, keepdims=True))
            e_c = jnp.exp(s_c - mx)
            den = jnp.sum(e_c, axis=-1, keepdims=True) + jnp.exp(sink - mx)
            if latent:
                e_b = jnp.exp(s_b - mx)
                den = den + jnp.sum(e_b, axis=-1, keepdims=True)
            o = _dot((e_c / den).astype(BF16), vc)
            if latent:
                o = o + _dot((e_b / den).astype(BF16), vb)
            o_ref[:, cols] = o.astype(BF16)


def _win_call(qkv, sink, *, L, nb, row_blk0, qb, gb, cache=None, tables=None, o_prev=None):
    latent = cache is not None
    nq = L // qb
    ng = WIN_KV // gb
    qw = gb * (WIN_H // WIN_KV) * DH
    koff = WIN_H // gb
    voff = (WIN_H + WIN_KV) // gb
    rq = lambda b, qi: (row_blk0 + b) * nq + qi
    rs = lambda b: row_blk0 + b
    in_specs = [
        pl.BlockSpec((qb, qw), lambda b, g, qi: (rq(b, qi), g)),
        pl.BlockSpec((L, gb * DH), lambda b, g, qi: (rs(b), koff + g)),
        pl.BlockSpec((L, gb * DH), lambda b, g, qi: (rs(b), voff + g)),
    ]
    args = [qkv] * 3
    scratch = []
    if latent:
        ck, cv = cache
        in_specs += [pl.BlockSpec((1, PAST, gb * DH), lambda b, g, qi: (b, 0, g))] * 2
        args += [ck, cv]
        in_specs += [pl.BlockSpec((qb, DH), lambda b, g, qi: (qi, 0))] * 3
        in_specs += [pl.BlockSpec((L, DH), lambda b, g, qi: (0, 0))] * 3
        args += list(tables) * 2
        kt = PAST + L + 2 * WINDOW
        scratch = [pltpu.VMEM((gb, kt, DH), BF16), pltpu.VMEM((gb, kt, DH), BF16)]
    in_specs.append(pl.BlockSpec(memory_space=pltpu.SMEM))
    args.append(sink)
    body = functools.partial(_win_kernel, L=L, latent=latent, gb=gb)
    aliases = {}
    if o_prev is not None:
        in_specs.append(pl.BlockSpec(memory_space=pl.ANY))
        args.append(o_prev)
        aliases = {len(args) - 1: 0}
        body = _drop_ref(body, len(args) - 1)
    out_specs = [pl.BlockSpec((qb, qw), lambda b, g, qi: (rq(b, qi), g))]
    out_shape = [jax.ShapeDtypeStruct((NT, D), BF16)]
    if not latent:
        assert gb == WIN_KV and nq == 1 and row_blk0 == 0
        out_specs += [pl.BlockSpec((L, gb * DH), lambda b, g, qi: (b, 0))] * 2
        out_shape += [jax.ShapeDtypeStruct((nb * L, gb * DH), F32)] * 2
    return pl.pallas_call(
        body,
        grid=(nb, ng, nq),
        in_specs=in_specs,
        out_specs=out_specs,
        out_shape=out_shape,
        scratch_shapes=scratch,
        input_output_aliases=aliases,
        compiler_params=_cparams(("parallel", "parallel", "arbitrary")),
        name="win_attn",
    )(*args)


def kernel(x_prompt, x_sample, state_l0_fwd, state_l0_bwd, cache_l1_k, cache_l1_v, cache_l2_k, cache_l2_v, state_l3_fwd, state_l3_bwd, c, c_ctx, norm_g, w_ada, b_ada, ffn1_w_in, ffn1_w_out, ffn2_w_in, ffn2_w_out, ssm_w_in, ssm_conv_w, ssm_conv_b, ssm_dt_bias, ssm_a_log, ssm_d, ssm_norm_g, ssm_w_out, diff_w_qkv, diff_lambda, diff_subln_g, diff_w_out, win_w_qkv, win_sink, win_w_out, final_norm_g):
    x = _stack_tokens(x_prompt.reshape(NP, D), x_sample.reshape(NS, D))
    cv8 = jnp.concatenate([c_ctx[None, :], c, jnp.zeros((8 - 1 - NB_S, D), F32)], axis=0)
    mods = _modulation(cv8, w_ada, b_ada)
    norm_g4 = norm_g.reshape(DEPTH, 3, 1, D)
    tables = _rope_tables()
    ssm_states = [(state_l0_fwd, state_l0_bwd), (state_l3_fwd, state_l3_bwd)]
    new_state = []
    for l in range(DEPTH):
        m, j = l % 3, l // 3
        x = _ffn(x, mods, norm_g4, ffn1_w_in, ffn1_w_out, l, 0)
        if m == 0:
            zx, dt_raw = _inproj(x, mods, norm_g4, ssm_w_in, l, j, 512, n_main=D_INNER + CONV_CH)
            y, hf, hb = _ssm_mixer(zx, dt_raw, *ssm_states[j], ssm_conv_w[j], ssm_conv_b[j],
                                   ssm_dt_bias[j], ssm_a_log[j], ssm_d[j])
            x = _outproj(y, ssm_w_out, j, x, mods, l, norm_g=ssm_norm_g[j])
            new_state += [hf, hb]
        elif m == 1:
            lambda_init = 0.8 - 0.6 * math.exp(-0.3 * l)
            (qkv,) = _inproj(x, mods, norm_g4, diff_w_qkv, l, j, 512)
            o, k_new, v_new = _diff_call(qkv, diff_lambda[j], diff_subln_g[j], lambda_init,
                                         L=L_P, nb=NB_P, row_blk0=0, qb=L_P, hb=DIFF_H)
            cache = (cache_l1_k.reshape(NB_S, PAST, D), cache_l1_v.reshape(NB_S, PAST, D))
            (o,) = _diff_call(qkv, diff_lambda[j], diff_subln_g[j], lambda_init,
                              L=L_S, nb=NB_S, row_blk0=NP // L_S, qb=256, hb=4, cache=cache,
                              tables=tables, o_prev=o)
            x = _outproj(o, diff_w_out, j, x, mods, l)
            new_state += [k_new.reshape(NB_P, L_P, 2, DIFF_H, DH),
                          v_new.reshape(NB_P, L_P, DIFF_H, 2 * DH)]
        else:
            (qkv,) = _inproj(x, mods, norm_g4, win_w_qkv, l, j, 512)
            o, k_new, v_new = _win_call(qkv, win_sink[j], L=L_P, nb=NB_P, row_blk0=0, qb=L_P,
                                        gb=WIN_KV)
            kvw = WIN_KV * DH
            cache = (cache_l2_k.reshape(NB_S, PAST, kvw), cache_l2_v.reshape(NB_S, PAST, kvw))
            (o,) = _win_call(qkv, win_sink[j], L=L_S, nb=NB_S, row_blk0=NP // L_S, qb=WINDOW, gb=4,
                             cache=cache, tables=tables, o_prev=o)
            x = _outproj(o, win_w_out, j, x, mods, l)
            new_state += [k_new.reshape(NB_P, L_P, WIN_KV, DH), v_new.reshape(NB_P, L_P, WIN_KV, DH)]
        x = _ffn(x, mods, norm_g4, ffn2_w_in, ffn2_w_out, l, 2)
    y_p = _final_norm(x, final_norm_g, 0, NP)
    y_s = _final_norm(x, final_norm_g, NP, NS)
    s0f, s0b, k1, v1, k2, v2, s3f, s3b = new_state
    return (y_p.reshape(NB_P, L_P, D), y_s.reshape(NB_S, L_S, D),
            s0f, s0b, k1, v1, k2, v2, s3f, s3b)
```
